```python
import jax, jax.numpy as jnp
from jax import lax
import numpy as np

D_MODEL = 2048
BATCH = 1
SEQ = 16384
DEPTH = 1

HEAD_DIM = 128
MLA_HEADS = D_MODEL // (2 * HEAD_DIM)
MOBA_HEADS = D_MODEL // (2 * HEAD_DIM)
MLA_NOPE = 128
MLA_ROPE = 64
MLA_V = 128
KV_RANK = 512
MOBA_BLOCK = 256
MOBA_TOPK = 3
Q_CHUNK = 128
MOBA_Q_CHUNK = 64
PLE_DIM = 256
ROPE_THETA = 10000.0
EPS = 1e-6
NEG = -1e30
D_FF = -(-8 * D_MODEL // (3 * 256)) * 256

MLA_Q_COLS = MLA_HEADS * (MLA_NOPE + MLA_ROPE)
MOBA_W = MOBA_HEADS * HEAD_DIM
MLA_OUT = MLA_HEADS * MLA_V
MIX_WIDTH = MLA_OUT + MOBA_W
IN_COLS = MLA_Q_COLS + KV_RANK + MLA_ROPE + 3 * MOBA_W

kernel_name = "hybrid_mla_moba_parallel_heads"


def rmsnorm(x, g):
    xf = x.astype(jnp.float32)
    y = xf * lax.rsqrt(jnp.mean(xf * xf, axis=-1, keepdims=True) + EPS)
    return (y * g.astype(jnp.float32)).astype(x.dtype)


def rope(x, pos):
    d = x.shape[-1]
    half = d // 2
    inv = ROPE_THETA ** (-(jnp.arange(half, dtype=jnp.float32) * 2.0 / d))
    ang = pos.astype(jnp.float32)[:, :, None, None] * inv
    cos, sin = jnp.cos(ang), jnp.sin(ang)
    x1 = x[..., :half].astype(jnp.float32)
    x2 = x[..., half:].astype(jnp.float32)
    return jnp.concatenate([x1 * cos - x2 * sin, x2 * cos + x1 * sin], axis=-1).astype(x.dtype)


def mla_attention(q_nope, q_pe, k_nope, k_pe, v):
    B, S, H, _ = q_nope.shape
    nc = S // Q_CHUNK
    scale = (MLA_NOPE + MLA_ROPE) ** -0.5

    def to_chunks(t):
        return t.reshape(B, nc, Q_CHUNK, H, t.shape[-1]).transpose(1, 0, 3, 2, 4)

    kn = k_nope.transpose(0, 2, 1, 3)
    vv = v.transpose(0, 2, 1, 3)
    key_idx = jnp.arange(S)

    def step(args):
        qn, qp, c = args
        s = (jnp.einsum('bhqd,bhkd->bhqk', qn, kn, preferred_element_type=jnp.float32)
             + jnp.einsum('bhqr,bkr->bhqk', qp, k_pe, preferred_element_type=jnp.float32)) * scale
        qpos = c * Q_CHUNK + jnp.arange(Q_CHUNK)
        s = jnp.where(key_idx[None, :] <= qpos[:, None], s, NEG)
        prob = jax.nn.softmax(s, axis=-1)
        return jnp.einsum('bhqk,bhkd->bhqd', prob.astype(vv.dtype), vv)

    out = lax.map(step, (to_chunks(q_nope), to_chunks(q_pe), jnp.arange(nc)))
    return out.transpose(1, 0, 3, 2, 4).reshape(B, S, H * MLA_V)


def moba_attention(q, k, v):
    B, S, H, dh = q.shape
    nb = -(-S // MOBA_BLOCK)
    pad = nb * MOBA_BLOCK - S
    scale = dh ** -0.5
    qh = q.transpose(0, 2, 1, 3)
    kp = jnp.pad(k.transpose(0, 2, 1, 3), ((0, 0), (0, 0), (0, pad), (0, 0)))
    vp = jnp.pad(v.transpose(0, 2, 1, 3), ((0, 0), (0, 0), (0, pad), (0, 0)))
    kb = kp.reshape(B, H, nb, MOBA_BLOCK, dh)
    vb = vp.reshape(B, H, nb, MOBA_BLOCK, dh)

    counts = jnp.clip(S - jnp.arange(nb) * MOBA_BLOCK, 1, MOBA_BLOCK).astype(jnp.float32)
    kmean = kb.astype(jnp.float32).sum(axis=3) / counts[:, None]
    gate = jnp.einsum('bhsd,bhnd->bhsn', qh.astype(jnp.float32), kmean)
    qblk = jnp.arange(S) // MOBA_BLOCK
    gate = jnp.where(jnp.arange(nb)[None, :] < qblk[:, None], gate, NEG)
    ksel = min(MOBA_TOPK, nb)
    _, idx = lax.top_k(gate, ksel)

    nc = S // MOBA_Q_CHUNK
    qc_all = qh.reshape(B, H, nc, MOBA_Q_CHUNK, dh).transpose(2, 0, 1, 3, 4)
    ic_all = idx.reshape(B, H, nc, MOBA_Q_CHUNK, ksel).transpose(2, 0, 1, 3, 4)
    bi = jnp.arange(B)[:, None, None, None]
    hi = jnp.arange(H)[None, :, None, None]

    def step(args):
        qc, ic, c = args
        qpos = c * MOBA_Q_CHUNK + jnp.arange(MOBA_Q_CHUNK)
        blk = (c * MOBA_Q_CHUNK) // MOBA_BLOCK
        k_sel = kb[bi, hi, ic]
        v_sel = vb[bi, hi, ic]
        s_sel = jnp.einsum('bhqd,bhqnjd->bhqnj', qc, k_sel, preferred_element_type=jnp.float32) * scale
        valid = jnp.arange(ksel)[None, :] < (qpos // MOBA_BLOCK)[:, None]
        s_sel = jnp.where(valid[:, :, None], s_sel, NEG).reshape(B, H, MOBA_Q_CHUNK, ksel * MOBA_BLOCK)
        k_own = lax.dynamic_slice_in_dim(kp, blk * MOBA_BLOCK, MOBA_BLOCK, axis=2)
        v_own = lax.dynamic_slice_in_dim(vp, blk * MOBA_BLOCK, MOBA_BLOCK, axis=2)
        s_own = jnp.einsum('bhqd,bhjd->bhqj', qc, k_own, preferred_element_type=jnp.float32) * scale
        own_pos = blk * MOBA_BLOCK + jnp.arange(MOBA_BLOCK)
        s_own = jnp.where(own_pos[None, :] <= qpos[:, None], s_own, NEG)
        prob = jax.nn.softmax(jnp.concatenate([s_sel, s_own], axis=-1), axis=-1)
        p_sel = prob[..., :ksel * MOBA_BLOCK].reshape(B, H, MOBA_Q_CHUNK, ksel, MOBA_BLOCK)
        p_own = prob[..., ksel * MOBA_BLOCK:]
        return (jnp.einsum('bhqnj,bhqnjd->bhqd', p_sel.astype(v_sel.dtype), v_sel)
                + jnp.einsum('bhqj,bhjd->bhqd', p_own.astype(v_own.dtype), v_own))

    out = lax.map(step, (qc_all, ic_all, jnp.arange(nc)))
    return out.transpose(1, 0, 3, 2, 4).reshape(B, S, H * dh)


def setup_inputs(seed: int = 0) -> dict:
    key = jax.random.key(seed)
    ks = jax.random.split(key, 20)
    f32 = jnp.float32

    def w(k, shape, fan_in):
        return jax.random.normal(k, shape, f32) * (fan_in ** -0.5)

    def gain(k, shape):
        return 1.0 + 0.02 * jax.random.normal(k, shape, f32)

    x = jax.random.normal(ks[0], (BATCH, SEQ, D_MODEL), f32)
    p = jax.random.normal(ks[1], (DEPTH, BATCH, SEQ, PLE_DIM), f32)
    positions = jnp.broadcast_to(jnp.arange(SEQ, dtype=jnp.int32)[None, :], (BATCH, SEQ))
    return {
        "x": x,
        "p": p,
        "positions": positions,
        "attn_norm": gain(ks[2], (DEPTH, D_MODEL)),
        "w_in": w(ks[3], (DEPTH, D_MODEL, IN_COLS), D_MODEL),
        "kv_norm": gain(ks[4], (DEPTH, KV_RANK)),
        "w_ukv": w(ks[5], (DEPTH, KV_RANK, MLA_HEADS * (MLA_NOPE + MLA_V)), KV_RANK),
        "w_o": w(ks[6], (DEPTH, MIX_WIDTH, D_MODEL), MIX_WIDTH),
        "ffn_norm": gain(ks[7], (DEPTH, D_MODEL)),
        "w_gate": w(ks[8], (DEPTH, D_MODEL, D_FF), D_MODEL),
        "w_up": w(ks[9], (DEPTH, D_MODEL, D_FF), D_MODEL),
        "w_down": w(ks[10], (DEPTH, D_FF, D_MODEL), D_FF),
        "ple_norm": gain(ks[11], (DEPTH, D_MODEL)),
        "w_ple_gate": w(ks[12], (DEPTH, D_MODEL, D_MODEL), D_MODEL),
        "w_ple_proj": w(ks[13], (DEPTH, PLE_DIM, D_MODEL), PLE_DIM),
        "final_norm": gain(ks[14], (D_MODEL,)),
    }


def reference(x, p, positions, attn_norm, w_in, kv_norm, w_ukv, w_o, ffn_norm,
              w_gate, w_up, w_down, ple_norm, w_ple_gate, w_ple_proj, final_norm):
    B, S, _ = x.shape
    splits = [MLA_Q_COLS, MLA_Q_COLS + KV_RANK, MLA_Q_COLS + KV_RANK + MLA_ROPE,
              MLA_Q_COLS + KV_RANK + MLA_ROPE + MOBA_W, MLA_Q_COLS + KV_RANK + MLA_ROPE + 2 * MOBA_W]
    h = x
    for i in range(DEPTH):
        a = rmsnorm(h, attn_norm[i])
        proj = a @ w_in[i]
        q_mla, c_kv, k_pe, q_mb, k_mb, v_mb = jnp.split(proj, splits, axis=-1)

        q_mla = q_mla.reshape(B, S, MLA_HEADS, MLA_NOPE + MLA_ROPE)
        q_nope, q_pe = q_mla[..., :MLA_NOPE], q_mla[..., MLA_NOPE:]
        q_pe = rope(q_pe, positions)
        k_pe = rope(k_pe[:, :, None, :], positions)[:, :, 0, :]
        kv = (rmsnorm(c_kv, kv_norm[i]) @ w_ukv[i]).reshape(B, S, MLA_HEADS, MLA_NOPE + MLA_V)
        k_nope, v_mla = kv[..., :MLA_NOPE], kv[..., MLA_NOPE:]
        out_mla = mla_attention(q_nope, q_pe, k_nope, k_pe, v_mla)

        q_mb = rope(q_mb.reshape(B, S, MOBA_HEADS, HEAD_DIM), positions)
        k_mb = rope(k_mb.reshape(B, S, MOBA_HEADS, HEAD_DIM), positions)
        v_mb = v_mb.reshape(B, S, MOBA_HEADS, HEAD_DIM)
        out_moba = moba_attention(q_mb, k_mb, v_mb)

        h = h + jnp.concatenate([out_mla, out_moba], axis=-1) @ w_o[i]

        f = rmsnorm(h, ffn_norm[i])
        h = h + (jax.nn.silu(f @ w_gate[i]) * (f @ w_up[i])) @ w_down[i]

        g = jax.nn.sigmoid(rmsnorm(h, ple_norm[i]) @ w_ple_gate[i])
        h = h + g * (p[i] @ w_ple_proj[i])
    return rmsnorm(h, final_norm)
```

```python
import functools

import jax
import jax.numpy as jnp
from jax import lax
from jax.experimental import pallas as pl
from jax.experimental.pallas import tpu as pltpu

F32 = jnp.float32
BF16 = jnp.bfloat16

EPS = 1e-6
NEG = -1e30
M_INIT = -3.0e38
ROPE_THETA = 10000.0

LANES = 128
HEAD_DIM = 128
MLA_ROPE = 64
KV_RANK = 512
N_HEADS = 8
MOBA_BLOCK = 256
MOBA_TOPK = 3
QK_WIDTH = 2 * LANES
VMEM_LIMIT = 56 * 1024 * 1024


def _rms(x, g):
    ms = jnp.mean(x * x, axis=-1, keepdims=True)
    return x * lax.rsqrt(ms + EPS) * g


def _rope(slab, cos, sin):
    return slab * cos + pltpu.roll(slab, LANES // 2, axis=1) * sin


def _rope_table_kernel(pos_ref, inv_ref, cos_ref, sin_ref):
    pos = pos_ref[...].astype(F32)
    lane = lax.broadcasted_iota(jnp.int32, (1, LANES), 1)
    sign = jnp.where(lane < LANES // 2, -1.0, 1.0).astype(F32)
    for t in range(2):
        ang = pos * inv_ref[t:t + 1, :]
        cos_ref[t] = jnp.cos(ang)
        sin_ref[t] = jnp.sin(ang) * sign


def _rope_tables(positions, bm=1024):
    s = positions.shape[-1]
    pos = positions.reshape(s, 1)
    half_mb = HEAD_DIM // 2
    half_ml = MLA_ROPE // 2
    inv_mb = ROPE_THETA ** (-(jnp.arange(half_mb, dtype=F32) * 2.0 / HEAD_DIM))
    inv_ml = ROPE_THETA ** (-(jnp.arange(half_ml, dtype=F32) * 2.0 / MLA_ROPE))
    inv = jnp.stack([jnp.tile(inv_mb, LANES // half_mb), jnp.tile(inv_ml, LANES // half_ml)])
    out = jax.ShapeDtypeStruct((2, s, LANES), F32)
    return pl.pallas_call(
        _rope_table_kernel,
        grid=(s // bm,),
        in_specs=[pl.BlockSpec((bm, 1), lambda i: (i, 0)),
                  pl.BlockSpec((2, LANES), lambda i: (0, 0))],
        out_specs=[pl.BlockSpec((2, bm, LANES), lambda i: (0, i, 0))] * 2,
        out_shape=[out, out],
        name="rope_tables",
    )(pos, inv)


def _proj_kernel(x_ref, g_ref, w_ref, cos_ref, sin_ref, o_ref, km_ref, a_scr, *, scale_mla, scale_mb):
    j = pl.program_id(1)
    bm = x_ref.shape[0]

    @pl.when(j == 0)
    def _():
        a_scr[...] = _rms(x_ref[...], g_ref[...]).astype(BF16)

    acc = jnp.dot(a_scr[...], w_ref[...], preferred_element_type=F32)

    def head(h):
        return acc[:, h * LANES:(h + 1) * LANES]

    @pl.when(j < 2)
    def _():
        for h in range(4):
            o_ref[:, (2 * h) * LANES:(2 * h + 1) * LANES] = (head(2 * h) * scale_mla).astype(BF16)
            r = _rope(head(2 * h + 1), cos_ref[1], sin_ref[1])
            o_ref[:, (2 * h + 1) * LANES:(2 * h + 2) * LANES] = (r * scale_mla).astype(BF16)

    @pl.when(j == 2)
    def _():
        for h in range(N_HEADS):
            r = _rope(head(h), cos_ref[0], sin_ref[0])
            o_ref[:, h * LANES:(h + 1) * LANES] = (r * scale_mb).astype(BF16)

    @pl.when(j == 3)
    def _():
        nb = bm // MOBA_BLOCK
        for h in range(N_HEADS):
            r = _rope(head(h), cos_ref[0], sin_ref[0])
            o_ref[:, h * LANES:(h + 1) * LANES] = r.astype(BF16)
            km = r.reshape(nb, MOBA_BLOCK, LANES).sum(axis=1) * (1.0 / MOBA_BLOCK)
            km_ref[0, :, h * LANES:(h + 1) * LANES] = km

    @pl.when(j == 4)
    def _():
        o_ref[...] = acc.astype(BF16)


def _in_proj(x2, g, w, cos, sin, bm=1024, bn=1024):
    s, d = x2.shape
    n = w.shape[1]
    nb = bm // MOBA_BLOCK
    kern = functools.partial(_proj_kernel, scale_mla=float((HEAD_DIM + MLA_ROPE) ** -0.5),
                             scale_mb=float(HEAD_DIM ** -0.5))
    return pl.pallas_call(
        kern,
        grid=(s // bm, n // bn),
        in_specs=[pl.BlockSpec((bm, d), lambda i, j: (i, 0)),
                  pl.BlockSpec((1, d), lambda i, j: (0, 0)),
                  pl.BlockSpec((d, bn), lambda i, j: (0, j)),
                  pl.BlockSpec((2, bm, LANES), lambda i, j: (0, i, 0)),
                  pl.BlockSpec((2, bm, LANES), lambda i, j: (0, i, 0))],
        out_specs=[pl.BlockSpec((bm, bn), lambda i, j: (i, j)),
                   pl.BlockSpec((1, nb, bn), lambda i, j: (i, 0, 0))],
        out_shape=[jax.ShapeDtypeStruct((s, n), BF16),
                   jax.ShapeDtypeStruct((s // bm, nb, bn), F32)],
        scratch_shapes=[pltpu.VMEM((bm, d), BF16)],
        compiler_params=pltpu.CompilerParams(
            dimension_semantics=("arbitrary", "arbitrary"), vmem_limit_bytes=VMEM_LIMIT),
        name="in_proj",
    )(x2, g, w, cos, sin)


def _mla_kv_kernel(x_ref, g_ref, wc_ref, kvg_ref, wuk_ref, wuv_ref, cos_ref, sin_ref, k_ref, v_ref):
    a = _rms(x_ref[...], g_ref[...]).astype(BF16)
    t = jnp.dot(a, wc_ref[...], preferred_element_type=F32)
    cn = _rms(t[:, :KV_RANK], kvg_ref[...]).astype(BF16)
    kn = jnp.dot(cn, wuk_ref[...], preferred_element_type=F32)
    vv = jnp.dot(cn, wuv_ref[...], preferred_element_type=F32)
    kpe = _rope(t[:, KV_RANK:], cos_ref[0], sin_ref[0]).astype(BF16)
    for h in range(N_HEADS):
        k_ref[:, (2 * h) * LANES:(2 * h + 1) * LANES] = kn[:, h * LANES:(h + 1) * LANES].astype(BF16)
        k_ref[:, (2 * h + 1) * LANES:(2 * h + 2) * LANES] = kpe
    v_ref[...] = vv.astype(BF16)


def _mla_kv(x2, g, wc, kvg, wuk, wuv, cos, sin, bm=512):
    s, d = x2.shape
    nc = wc.shape[1]
    full = lambda a: pl.BlockSpec(a.shape, lambda i: (0,) * a.ndim)
    return pl.pallas_call(
        _mla_kv_kernel,
        grid=(s // bm,),
        in_specs=[pl.BlockSpec((bm, d), lambda i: (i, 0)), full(g), full(wc), full(kvg), full(wuk), full(wuv),
                  pl.BlockSpec((1, bm, LANES), lambda i: (1, i, 0)),
                  pl.BlockSpec((1, bm, LANES), lambda i: (1, i, 0))],
        out_specs=[pl.BlockSpec((bm, N_HEADS * QK_WIDTH), lambda i: (i, 0)),
                   pl.BlockSpec((bm, N_HEADS * HEAD_DIM), lambda i: (i, 0))],
        out_shape=[jax.ShapeDtypeStruct((s, N_HEADS * QK_WIDTH), BF16),
                   jax.ShapeDtypeStruct((s, N_HEADS * HEAD_DIM), BF16)],
        compiler_params=pltpu.CompilerParams(
            dimension_semantics=("arbitrary",), vmem_limit_bytes=VMEM_LIMIT),
        name="mla_kv",
    )(x2, g, wc, kvg, wuk, wuv, cos, sin)


def _gate_kernel(q_ref, k_ref, km_ref, qa_ref, ka_ref):
    i = pl.program_id(0)
    bm = q_ref.shape[0]
    q = q_ref[...]
    km = km_ref[...]
    nblk = km.shape[0]
    km_pad = jnp.concatenate([km, jnp.zeros((LANES - nblk, LANES), F32)], axis=0).astype(BF16)
    gate = lax.dot_general(q, km_pad, (((1,), (1,)), ((), ())), preferred_element_type=F32)
    col = lax.broadcasted_iota(jnp.int32, (bm, LANES), 1)
    row = i * bm + lax.broadcasted_iota(jnp.int32, (bm, LANES), 0)
    blk = lax.shift_right_logical(row, 8)
    colf = col.astype(F32)
    g = jnp.where(col < blk, gate, NEG)
    sel = jnp.where(col == blk, 1.0, 0.0)
    for t in range(MOBA_TOPK):
        m = jnp.max(g, axis=-1, keepdims=True)
        idx = jnp.min(jnp.where(g == m, colf, float(LANES)), axis=-1, keepdims=True)
        pick = colf == idx
        sel = jnp.where(pick, jnp.where(blk > t, 1.0, sel), sel)
        g = jnp.where(pick, -jnp.inf, g)
    qa_ref[:, :LANES] = q
    qa_ref[:, LANES:] = jnp.where(sel > 0.5, 0.0, NEG).astype(BF16)
    ka_ref[:, :LANES] = k_ref[...]
    ka_ref[:, LANES:] = jnp.where(col == blk, 1.0, 0.0).astype(BF16)


def _moba_gate(proj, kmean, q_col0, k_col0, bm=512):
    s = proj.shape[0]
    nblk = kmean.shape[0]
    qb, kb = q_col0 // LANES, k_col0 // LANES
    out = jax.ShapeDtypeStruct((s, N_HEADS * QK_WIDTH), BF16)
    return pl.pallas_call(
        _gate_kernel,
        grid=(s // bm, N_HEADS),
        in_specs=[pl.BlockSpec((bm, LANES), lambda i, h: (i, qb + h)),
                  pl.BlockSpec((bm, LANES), lambda i, h: (i, kb + h)),
                  pl.BlockSpec((nblk, LANES), lambda i, h: (0, h))],
        out_specs=[pl.BlockSpec((bm, QK_WIDTH), lambda i, h: (i, h))] * 2,
        out_shape=[out, out],
        compiler_params=pltpu.CompilerParams(dimension_semantics=("arbitrary", "arbitrary")),
        name="moba_gate",
    )(proj, proj, kmean)


def _flash_kernel(q_ref, k_ref, v_ref, o_ref, m_scr, l_scr, acc_scr, *, tq, tk):
    qi = pl.program_id(1)
    q = q_ref[...]
    m_scr[...] = jnp.full(m_scr.shape, M_INIT, F32)
    l_scr[...] = jnp.zeros(l_scr.shape, F32)
    acc_scr[...] = jnp.zeros(acc_scr.shape, F32)
    reps = tk // LANES

    def step(kj, masked):
        start = pl.multiple_of(kj * tk, tk)
        k = k_ref[pl.ds(start, tk), :]
        v = v_ref[pl.ds(start, tk), :]
        s = lax.dot_general(q, k, (((1,), (1,)), ((), ())), preferred_element_type=F32)
        if masked:
            rows = qi * tq + lax.broadcasted_iota(jnp.int32, (tq, tk), 0)
            cols = kj * tk + lax.broadcasted_iota(jnp.int32, (tq, tk), 1)
            s = jnp.where(cols <= rows, s, NEG)
        m_prev = m_scr[...]
        m_new = jnp.maximum(m_prev, jnp.max(s, axis=-1, keepdims=True))
        alpha = jnp.exp(m_prev - m_new)
        p = jnp.exp(s - jnp.concatenate([m_new] * reps, axis=1))
        l_scr[...] = alpha * l_scr[...] + jnp.sum(p, axis=-1, keepdims=True)
        acc_scr[...] = alpha * acc_scr[...] + jnp.dot(p.astype(BF16), v, preferred_element_type=F32)
        m_scr[...] = m_new

    n_full = (qi * tq) // tk

    def body(kj, c):
        step(kj, False)
        return c

    lax.fori_loop(0, n_full, body, 0)
    for d in range(tq // tk):
        step(n_full + d, True)
    o_ref[...] = (acc_scr[...] / l_scr[...]).astype(o_ref.dtype)


def _flash(q, k, v, q_col0=0, v_col0=0, tq=512, tk=512):
    s = q.shape[0]
    qb, vb = q_col0 // QK_WIDTH, v_col0 // HEAD_DIM
    kern = functools.partial(_flash_kernel, tq=tq, tk=tk)
    return pl.pallas_call(
        kern,
        grid=(N_HEADS, s // tq),
        in_specs=[pl.BlockSpec((tq, QK_WIDTH), lambda h, i: (i, qb + h)),
                  pl.BlockSpec((s, QK_WIDTH), lambda h, i: (0, h)),
                  pl.BlockSpec((s, HEAD_DIM), lambda h, i: (0, vb + h))],
        out_specs=pl.BlockSpec((tq, HEAD_DIM), lambda h, i: (i, h)),
        out_shape=jax.ShapeDtypeStruct((s, N_HEADS * HEAD_DIM), BF16),
        scratch_shapes=[pltpu.VMEM((tq, LANES), F32), pltpu.VMEM((tq, LANES), F32),
                        pltpu.VMEM((tq, HEAD_DIM), F32)],
        compiler_params=pltpu.CompilerParams(
            dimension_semantics=("arbitrary", "arbitrary"), vmem_limit_bytes=VMEM_LIMIT),
        name="flash_attn",
    )(q, k, v)


def _wo_kernel(x_ref, a_ref, b_ref, wa_ref, wb_ref, o_ref):
    o_ref[...] = (x_ref[...]
                  + jnp.dot(a_ref[...], wa_ref[...], preferred_element_type=F32)
                  + jnp.dot(b_ref[...], wb_ref[...], preferred_element_type=F32))


def _out_proj(x2, oa, ob, wa, wb, bm=1024, bn=1024):
    s, d = x2.shape
    ka, kb = oa.shape[1], ob.shape[1]
    return pl.pallas_call(
        _wo_kernel,
        grid=(s // bm, d // bn),
        in_specs=[pl.BlockSpec((bm, bn), lambda i, j: (i, j)),
                  pl.BlockSpec((bm, ka), lambda i, j: (i, 0)),
                  pl.BlockSpec((bm, kb), lambda i, j: (i, 0)),
                  pl.BlockSpec((ka, bn), lambda i, j: (0, j)),
                  pl.BlockSpec((kb, bn), lambda i, j: (0, j))],
        out_specs=pl.BlockSpec((bm, bn), lambda i, j: (i, j)),
        out_shape=jax.ShapeDtypeStruct((s, d), F32),
        compiler_params=pltpu.CompilerParams(
            dimension_semantics=("arbitrary", "arbitrary"), vmem_limit_bytes=VMEM_LIMIT),
        name="out_proj",
    )(x2, oa, ob, wa, wb)


def _ffn_kernel(h_ref, g_ref, wg_ref, wu_ref, wd_ref, o_ref, f_scr):
    k = pl.program_id(1)

    @pl.when(k == 0)
    def _():
        h = h_ref[...]
        f_scr[...] = _rms(h, g_ref[...]).astype(BF16)
        o_ref[...] = h

    f = f_scr[...]
    gt = jnp.dot(f, wg_ref[...], preferred_element_type=F32)
    up = jnp.dot(f, wu_ref[...], preferred_element_type=F32)
    act = (gt * jax.nn.sigmoid(gt) * up).astype(BF16)
    o_ref[...] += jnp.dot(act, wd_ref[...], preferred_element_type=F32)


def _ffn(h, g, wg, wu, wd, bm=512, bf=512):
    s, d = h.shape
    f = wg.shape[1]
    return pl.pallas_call(
        _ffn_kernel,
        grid=(s // bm, f // bf),
        in_specs=[pl.BlockSpec((bm, d), lambda i, k: (i, 0)),
                  pl.BlockSpec((1, d), lambda i, k: (0, 0)),
                  pl.BlockSpec((d, bf), lambda i, k: (0, k)),
                  pl.BlockSpec((d, bf), lambda i, k: (0, k)),
                  pl.BlockSpec((bf, d), lambda i, k: (k, 0))],
        out_specs=pl.BlockSpec((bm, d), lambda i, k: (i, 0)),
        out_shape=jax.ShapeDtypeStruct((s, d), F32),
        scratch_shapes=[pltpu.VMEM((bm, d), BF16)],
        compiler_params=pltpu.CompilerParams(
            dimension_semantics=("arbitrary", "arbitrary"), vmem_limit_bytes=VMEM_LIMIT),
        name="swiglu",
    )(h, g, wg, wu, wd)


def _ple_kernel(h_ref, p_ref, gp_ref, wpg_ref, wpe_ref, gf_ref, o_ref):
    h = h_ref[...]
    r = _rms(h, gp_ref[...]).astype(BF16)
    gate = jax.nn.sigmoid(jnp.dot(r, wpg_ref[...], preferred_element_type=F32))
    pe = jnp.dot(p_ref[...].astype(BF16), wpe_ref[...], preferred_element_type=F32)
    o_ref[...] = _rms(h + gate * pe, gf_ref[...])


def _ple_final(h, p2, gp, wpg, wpe, gf, bm=512):
    s, d = h.shape
    pd = p2.shape[1]
    full = lambda a: pl.BlockSpec(a.shape, lambda i: (0,) * a.ndim)
    return pl.pallas_call(
        _ple_kernel,
        grid=(s // bm,),
        in_specs=[pl.BlockSpec((bm, d), lambda i: (i, 0)),
                  pl.BlockSpec((bm, pd), lambda i: (i, 0)),
                  full(gp), full(wpg), full(wpe), full(gf)],
        out_specs=pl.BlockSpec((bm, d), lambda i: (i, 0)),
        out_shape=jax.ShapeDtypeStruct((s, d), F32),
        compiler_params=pltpu.CompilerParams(
            dimension_semantics=("arbitrary",), vmem_limit_bytes=VMEM_LIMIT),
        name="ple_final",
    )(h, p2, gp, wpg, wpe, gf)


def _prep_in_weights(w_in):
    d = w_in.shape[0]
    h, nope, rope = N_HEADS, HEAD_DIM, MLA_ROPE
    half = rope // 2
    q_cols = h * (nope + rope)
    c0 = q_cols
    c1 = c0 + KV_RANK
    c2 = c1 + rope
    mw = h * HEAD_DIM
    z = lambda n: jnp.zeros((d, n), w_in.dtype)

    def rope_slab(w):
        return jnp.concatenate([w[:, :half], z(half), w[:, half:], z(half)], axis=1)

    wq = w_in[:, :q_cols].reshape(d, h, nope + rope)
    wq_heads = [jnp.concatenate([wq[:, i, :nope], rope_slab(wq[:, i, nope:])], axis=1) for i in range(h)]
    w_main = jnp.concatenate(wq_heads + [w_in[:, c2:c2 + 3 * mw]], axis=1).astype(BF16)
    w_ckv = jnp.concatenate([w_in[:, c0:c1], rope_slab(w_in[:, c1:c2])], axis=1).astype(BF16)
    return w_main, w_ckv


def kernel(x, p, positions, attn_norm, w_in, kv_norm, w_ukv, w_o, ffn_norm, w_gate, w_up, w_down,
           ple_norm, w_ple_gate, w_ple_proj, final_norm):
    b, s, d = x.shape
    assert b == 1 and p.shape[0] == 1 and s % 1024 == 0
    x2 = x.reshape(s, d)
    p2 = p.reshape(s, p.shape[-1])
    row = lambda v: v.reshape(1, -1).astype(F32)

    w_main, w_ckv = _prep_in_weights(w_in[0])
    wukv = w_ukv[0].reshape(KV_RANK, N_HEADS, 2 * HEAD_DIM)
    w_uk = wukv[:, :, :HEAD_DIM].reshape(KV_RANK, -1).astype(BF16)
    w_uv = wukv[:, :, HEAD_DIM:].reshape(KV_RANK, -1).astype(BF16)
    mla_out = N_HEADS * HEAD_DIM
    wo_a = w_o[0][:mla_out].astype(BF16)
    wo_b = w_o[0][mla_out:].astype(BF16)

    cos, sin = _rope_tables(positions)
    proj, kmean = _in_proj(x2, row(attn_norm[0]), w_main, cos, sin)
    k_mla, v_mla = _mla_kv(x2, row(attn_norm[0]), w_ckv, row(kv_norm[0]), w_uk, w_uv, cos, sin)

    mw = N_HEADS * HEAD_DIM
    q_mla_w = N_HEADS * QK_WIDTH
    kmean = kmean.reshape(s // MOBA_BLOCK, mw)
    q_aug, k_aug = _moba_gate(proj, kmean, q_mla_w, q_mla_w + mw)

    out_mla = _flash(proj, k_mla, v_mla)
    out_mb = _flash(q_aug, k_aug, proj, v_col0=q_mla_w + 2 * mw)

    h1 = _out_proj(x2, out_mla, out_mb, wo_a, wo_b)
    h2 = _ffn(h1, row(ffn_norm[0]), w_gate[0].astype(BF16), w_up[0].astype(BF16), w_down[0].astype(BF16))
    out = _ple_final(h2, p2, row(ple_norm[0]), w_ple_gate[0].astype(BF16), w_ple_proj[0].astype(BF16),
                     row(final_norm))
    return out.reshape(b, s, d)
```

```python
import functools

import jax
import jax.numpy as jnp
from jax import lax
from jax.experimental import pallas as pl
from jax.experimental.pallas import tpu as pltpu

F32 = jnp.float32
BF16 = jnp.bfloat16

EPS = 1e-6
NEG = -1e30
M_INIT = -3.0e38
ROPE_THETA = 10000.0

LANES = 128
HEAD_DIM = 128
MLA_ROPE = 64
KV_RANK = 512
N_HEADS = 8
MOBA_BLOCK = 256
MOBA_TOPK = 3
QK_WIDTH = 2 * LANES
VMEM_LIMIT = 56 * 1024 * 1024


def _rms(x, g):
    ms = jnp.mean(x * x, axis=-1, keepdims=True)
    return x * lax.rsqrt(ms + EPS) * g


def _rope(slab, cos, sin):
    return slab * cos + pltpu.roll(slab, LANES // 2, axis=1) * sin


def _rope_table_kernel(pos_ref, inv_ref, cos_ref, sin_ref):
    pos = pos_ref[...].astype(F32)
    lane = lax.broadcasted_iota(jnp.int32, (1, LANES), 1)
    sign = jnp.where(lane < LANES // 2, -1.0, 1.0).astype(F32)
    for t in range(2):
        ang = pos * inv_ref[t:t + 1, :]
        cos_ref[t] = jnp.cos(ang)
        sin_ref[t] = jnp.sin(ang) * sign


def _rope_tables(positions, bm=1024):
    s = positions.shape[-1]
    pos = positions.reshape(s, 1)
    half_mb = HEAD_DIM // 2
    half_ml = MLA_ROPE // 2
    inv_mb = ROPE_THETA ** (-(jnp.arange(half_mb, dtype=F32) * 2.0 / HEAD_DIM))
    inv_ml = ROPE_THETA ** (-(jnp.arange(half_ml, dtype=F32) * 2.0 / MLA_ROPE))
    inv = jnp.stack([jnp.tile(inv_mb, LANES // half_mb), jnp.tile(inv_ml, LANES // half_ml)])
    out = jax.ShapeDtypeStruct((2, s, LANES), F32)
    return pl.pallas_call(
        _rope_table_kernel,
        grid=(s // bm,),
        in_specs=[pl.BlockSpec((bm, 1), lambda i: (i, 0)),
                  pl.BlockSpec((2, LANES), lambda i: (0, 0))],
        out_specs=[pl.BlockSpec((2, bm, LANES), lambda i: (0, i, 0))] * 2,
        out_shape=[out, out],
        name="rope_tables",
    )(pos, inv)


def _proj_kernel(x_ref, g_ref, w_ref, wvt_ref, cos_ref, sin_ref, o_ref, km_ref, vt_ref, a_scr, *,
                 scale_mla, scale_mb):
    j = pl.program_id(1)
    bm = x_ref.shape[0]

    @pl.when(j == 0)
    def _():
        a_scr[...] = _rms(x_ref[...], g_ref[...]).astype(BF16)

    def heads():
        acc = jnp.dot(a_scr[...], w_ref[...], preferred_element_type=F32)
        return lambda h: acc[:, h * LANES:(h + 1) * LANES]

    @pl.when(j < 2)
    def _():
        head = heads()
        for h in range(4):
            o_ref[:, (2 * h) * LANES:(2 * h + 1) * LANES] = (head(2 * h) * scale_mla).astype(BF16)
            r = _rope(head(2 * h + 1), cos_ref[1], sin_ref[1])
            o_ref[:, (2 * h + 1) * LANES:(2 * h + 2) * LANES] = (r * scale_mla).astype(BF16)

    @pl.when(j == 2)
    def _():
        head = heads()
        for h in range(N_HEADS):
            r = _rope(head(h), cos_ref[0], sin_ref[0])
            o_ref[:, h * LANES:(h + 1) * LANES] = (r * scale_mb).astype(BF16)

    @pl.when(j == 3)
    def _():
        head = heads()
        nb = bm // MOBA_BLOCK
        for h in range(N_HEADS):
            r = _rope(head(h), cos_ref[0], sin_ref[0])
            o_ref[:, h * LANES:(h + 1) * LANES] = r.astype(BF16)
            km = r.reshape(nb, MOBA_BLOCK, LANES).sum(axis=1) * (1.0 / MOBA_BLOCK)
            km_ref[0, :, h * LANES:(h + 1) * LANES] = km

    @pl.when(j == 4)
    def _():
        vt = lax.dot_general(wvt_ref[...], a_scr[...], (((1,), (1,)), ((), ())), preferred_element_type=F32)
        vt_ref[...] = vt.astype(BF16)


def _in_proj(x2, g, w, wvt, cos, sin, bm=1024, bn=1024):
    s, d = x2.shape
    n = w.shape[1] + wvt.shape[0]
    nt = n // bn
    nb = bm // MOBA_BLOCK
    log2e = 1.4426950408889634
    kern = functools.partial(_proj_kernel, scale_mla=float((HEAD_DIM + MLA_ROPE) ** -0.5 * log2e),
                             scale_mb=float(HEAD_DIM ** -0.5 * log2e))
    return pl.pallas_call(
        kern,
        grid=(s // bm, nt),
        in_specs=[pl.BlockSpec((bm, d), lambda i, j: (i, 0)),
                  pl.BlockSpec((1, d), lambda i, j: (0, 0)),
                  pl.BlockSpec((d, bn), lambda i, j: (0, jnp.minimum(j, nt - 2))),
                  pl.BlockSpec((bn, d), lambda i, j: (0, 0)),
                  pl.BlockSpec((2, bm, LANES), lambda i, j: (0, i, 0)),
                  pl.BlockSpec((2, bm, LANES), lambda i, j: (0, i, 0))],
        out_specs=[pl.BlockSpec((bm, bn), lambda i, j: (i, jnp.minimum(j, nt - 2))),
                   pl.BlockSpec((1, nb, bn), lambda i, j: (i, 0, 0)),
                   pl.BlockSpec((bn, bm), lambda i, j: (0, i))],
        out_shape=[jax.ShapeDtypeStruct((s, n - bn), BF16),
                   jax.ShapeDtypeStruct((s // bm, nb, bn), F32),
                   jax.ShapeDtypeStruct((bn, s), BF16)],
        scratch_shapes=[pltpu.VMEM((bm, d), BF16)],
        compiler_params=pltpu.CompilerParams(
            dimension_semantics=("arbitrary", "arbitrary"), vmem_limit_bytes=VMEM_LIMIT),
        name="in_proj",
    )(x2, g, w, wvt, cos, sin)


def _mla_kv_kernel(x_ref, g_ref, wc_ref, kvg_ref, wuk_ref, wuvt_ref, cos_ref, sin_ref, k_ref, vt_ref):
    a = _rms(x_ref[...], g_ref[...]).astype(BF16)
    t = jnp.dot(a, wc_ref[...], preferred_element_type=F32)
    cn = _rms(t[:, :KV_RANK], kvg_ref[...]).astype(BF16)
    kn = jnp.dot(cn, wuk_ref[...], preferred_element_type=F32)
    vt = lax.dot_general(wuvt_ref[...], cn, (((1,), (1,)), ((), ())), preferred_element_type=F32)
    kpe = _rope(t[:, KV_RANK:], cos_ref[0], sin_ref[0]).astype(BF16)
    for h in range(N_HEADS):
        k_ref[:, (2 * h) * LANES:(2 * h + 1) * LANES] = kn[:, h * LANES:(h + 1) * LANES].astype(BF16)
        k_ref[:, (2 * h + 1) * LANES:(2 * h + 2) * LANES] = kpe
    vt_ref[...] = vt.astype(BF16)


def _mla_kv(x2, g, wc, kvg, wuk, wuv, cos, sin, bm=512):
    s, d = x2.shape
    nc = wc.shape[1]
    full = lambda a: pl.BlockSpec(a.shape, lambda i: (0,) * a.ndim)
    return pl.pallas_call(
        _mla_kv_kernel,
        grid=(s // bm,),
        in_specs=[pl.BlockSpec((bm, d), lambda i: (i, 0)), full(g), full(wc), full(kvg), full(wuk), full(wuv),
                  pl.BlockSpec((1, bm, LANES), lambda i: (1, i, 0)),
                  pl.BlockSpec((1, bm, LANES), lambda i: (1, i, 0))],
        out_specs=[pl.BlockSpec((bm, N_HEADS * QK_WIDTH), lambda i: (i, 0)),
                   pl.BlockSpec((N_HEADS * HEAD_DIM, bm), lambda i: (0, i))],
        out_shape=[jax.ShapeDtypeStruct((s, N_HEADS * QK_WIDTH), BF16),
                   jax.ShapeDtypeStruct((N_HEADS * HEAD_DIM, s), BF16)],
        compiler_params=pltpu.CompilerParams(
            dimension_semantics=("arbitrary",), vmem_limit_bytes=VMEM_LIMIT),
        name="mla_kv",
    )(x2, g, wc, kvg, wuk, wuv, cos, sin)


def _gate_kernel(q_ref, k_ref, km_ref, qa_ref, ka_ref):
    i = pl.program_id(0)
    bm = q_ref.shape[0]
    q = q_ref[...]
    km = km_ref[...]
    nblk = km.shape[0]
    km_pad = jnp.concatenate([km, jnp.zeros((LANES - nblk, LANES), F32)], axis=0).astype(BF16)
    gate = lax.dot_general(q, km_pad, (((1,), (1,)), ((), ())), preferred_element_type=F32)
    col = lax.broadcasted_iota(jnp.int32, (bm, LANES), 1)
    row = i * bm + lax.broadcasted_iota(jnp.int32, (bm, LANES), 0)
    blk = lax.shift_right_logical(row, 8)
    colf = col.astype(F32)
    g = jnp.where(col < blk, gate, NEG)
    sel = jnp.where(col == blk, 1.0, 0.0)
    for t in range(MOBA_TOPK):
        m = jnp.max(g, axis=-1, keepdims=True)
        idx = jnp.min(jnp.where(g == m, colf, float(LANES)), axis=-1, keepdims=True)
        pick = colf == idx
        sel = jnp.where(pick, jnp.where(blk > t, 1.0, sel), sel)
        g = jnp.where(pick, -jnp.inf, g)
    qa_ref[:, :LANES] = q
    qa_ref[:, LANES:] = jnp.where(sel > 0.5, 0.0, NEG).astype(BF16)
    ka_ref[:, :LANES] = k_ref[...]
    ka_ref[:, LANES:] = jnp.where(col == blk, 1.0, 0.0).astype(BF16)


def _moba_gate(proj, kmean, q_col0, k_col0, bm=512):
    s = proj.shape[0]
    nblk = kmean.shape[0]
    qb, kb = q_col0 // LANES, k_col0 // LANES
    out = jax.ShapeDtypeStruct((s, N_HEADS * QK_WIDTH), BF16)
    return pl.pallas_call(
        _gate_kernel,
        grid=(s // bm, N_HEADS),
        in_specs=[pl.BlockSpec((bm, LANES), lambda i, h: (i, qb + h)),
                  pl.BlockSpec((bm, LANES), lambda i, h: (i, kb + h)),
                  pl.BlockSpec((nblk, LANES), lambda i, h: (0, h))],
        out_specs=[pl.BlockSpec((bm, QK_WIDTH), lambda i, h: (i, h))] * 2,
        out_shape=[out, out],
        compiler_params=pltpu.CompilerParams(dimension_semantics=("arbitrary", "arbitrary")),
        name="moba_gate",
    )(proj, proj, kmean)


def _flash_kernel(q_ref, k_ref, vt_ref, o_ref, s_scr, m_scr, l_scr, acc_scr, *, ng, tk):
    qi = pl.program_id(1)
    tq = ng * tk
    m_scr[...] = jnp.full(m_scr.shape, M_INIT, F32)
    l_scr[...] = jnp.zeros(l_scr.shape, F32)
    acc_scr[...] = jnp.zeros(acc_scr.shape, F32)

    def start(buf, g, kj):
        k = k_ref[pl.ds(pl.multiple_of(kj * tk, tk), tk), :]
        qg = q_ref[g * tk:(g + 1) * tk, :]
        s_scr[buf] = lax.dot_general(k, qg, (((1,), (1,)), ((), ())), preferred_element_type=F32)

    def finish(buf, g, kj, masked):
        st = s_scr[buf]
        if masked:
            key = kj * tk + lax.broadcasted_iota(jnp.int32, (tk, tk), 0)
            qry = qi * tq + g * tk + lax.broadcasted_iota(jnp.int32, (tk, tk), 1)
            st = jnp.where(key <= qry, st, NEG)
        m_prev = m_scr[g]
        m_new = jnp.maximum(m_prev, jnp.max(st, axis=0, keepdims=True))
        alpha = jnp.exp2(m_prev - m_new)
        p = jnp.exp2(st - m_new)
        l_scr[g] = alpha * l_scr[g] + jnp.sum(p, axis=0, keepdims=True)
        vt = vt_ref[:, pl.ds(pl.multiple_of(kj * tk, tk), tk)]
        acc_scr[g] = alpha * acc_scr[g] + jnp.dot(vt, p.astype(BF16), preferred_element_type=F32)
        m_scr[g] = m_new

    n_full = qi * ng
    tail = [(d, g, g == d) for d in range(ng) for g in range(d, ng)]

    start(0, 0, 0)

    def body(kj, c):
        for g in range(ng):
            if g + 1 < ng:
                start((g + 1) % 2, g + 1, kj)
            else:
                start(0, 0, kj + 1)
            finish(g % 2, g, kj, False)
        return c

    lax.fori_loop(0, n_full, body, 0)
    for i, (d, g, masked) in enumerate(tail):
        if i + 1 < len(tail):
            nd, ngp, _ = tail[i + 1]
            start((i + 1) % 2, ngp, n_full + nd)
        finish(i % 2, g, n_full + d, masked)
    for g in range(ng):
        o = acc_scr[g] / l_scr[g]
        o_ref[g * tk:(g + 1) * tk, :] = o.T.astype(o_ref.dtype)


def _flash(q, k, vt, q_col0=0, ng=4, tk=512):
    s = q.shape[0]
    assert ng % 2 == 0
    tq = ng * tk
    qb = q_col0 // QK_WIDTH
    kern = functools.partial(_flash_kernel, ng=ng, tk=tk)
    return pl.pallas_call(
        kern,
        grid=(N_HEADS, s // tq),
        in_specs=[pl.BlockSpec((tq, QK_WIDTH), lambda h, i: (i, qb + h)),
                  pl.BlockSpec((s, QK_WIDTH), lambda h, i: (0, h)),
                  pl.BlockSpec((HEAD_DIM, s), lambda h, i: (h, 0))],
        out_specs=pl.BlockSpec((tq, HEAD_DIM), lambda h, i: (i, h)),
        out_shape=jax.ShapeDtypeStruct((s, N_HEADS * HEAD_DIM), BF16),
        scratch_shapes=[pltpu.VMEM((2, tk, tk), F32),
                        pltpu.VMEM((ng, 1, tk), F32), pltpu.VMEM((ng, 1, tk), F32),
                        pltpu.VMEM((ng, HEAD_DIM, tk), F32)],
        compiler_params=pltpu.CompilerParams(
            dimension_semantics=("arbitrary", "arbitrary"), vmem_limit_bytes=VMEM_LIMIT),
        name="flash_attn",
    )(q, k, vt)


def _wo_kernel(x_ref, a_ref, b_ref, wa_ref, wb_ref, o_ref):
    o_ref[...] = (x_ref[...]
                  + jnp.dot(a_ref[...], wa_ref[...], preferred_element_type=F32)
                  + jnp.dot(b_ref[...], wb_ref[...], preferred_element_type=F32))


def _out_proj(x2, oa, ob, wa, wb, bm=1024, bn=1024):
    s, d = x2.shape
    ka, kb = oa.shape[1], ob.shape[1]
    return pl.pallas_call(
        _wo_kernel,
        grid=(s // bm, d // bn),
        in_specs=[pl.BlockSpec((bm, bn), lambda i, j: (i, j)),
                  pl.BlockSpec((bm, ka), lambda i, j: (i, 0)),
                  pl.BlockSpec((bm, kb), lambda i, j: (i, 0)),
                  pl.BlockSpec((ka, bn), lambda i, j: (0, j)),
                  pl.BlockSpec((kb, bn), lambda i, j: (0, j))],
        out_specs=pl.BlockSpec((bm, bn), lambda i, j: (i, j)),
        out_shape=jax.ShapeDtypeStruct((s, d), F32),
        compiler_params=pltpu.CompilerParams(
            dimension_semantics=("arbitrary", "arbitrary"), vmem_limit_bytes=VMEM_LIMIT),
        name="out_proj",
    )(x2, oa, ob, wa, wb)


def _ffn_kernel(h_ref, g_ref, wg_ref, wu_ref, wd_ref, o_ref, f_scr):
    k = pl.program_id(1)

    @pl.when(k == 0)
    def _():
        h = h_ref[...]
        f_scr[...] = _rms(h, g_ref[...]).astype(BF16)
        o_ref[...] = h

    f = f_scr[...]
    gt = jnp.dot(f, wg_ref[...], preferred_element_type=F32)
    up = jnp.dot(f, wu_ref[...], preferred_element_type=F32)
    act = (gt * jax.nn.sigmoid(gt) * up).astype(BF16)
    o_ref[...] += jnp.dot(act, wd_ref[...], preferred_element_type=F32)


def _ffn(h, g, wg, wu, wd, bm=512, bf=512):
    s, d = h.shape
    f = wg.shape[1]
    return pl.pallas_call(
        _ffn_kernel,
        grid=(s // bm, f // bf),
        in_specs=[pl.BlockSpec((bm, d), lambda i, k: (i, 0)),
                  pl.BlockSpec((1, d), lambda i, k: (0, 0)),
                  pl.BlockSpec((d, bf), lambda i, k: (0, k)),
                  pl.BlockSpec((d, bf), lambda i, k: (0, k)),
                  pl.BlockSpec((bf, d), lambda i, k: (k, 0))],
        out_specs=pl.BlockSpec((bm, d), lambda i, k: (i, 0)),
        out_shape=jax.ShapeDtypeStruct((s, d), F32),
        scratch_shapes=[pltpu.VMEM((bm, d), BF16)],
        compiler_params=pltpu.CompilerParams(
            dimension_semantics=("arbitrary", "arbitrary"), vmem_limit_bytes=VMEM_LIMIT),
        name="swiglu",
    )(h, g, wg, wu, wd)


def _ple_kernel(h_ref, p_ref, gp_ref, wpg_ref, wpe_ref, gf_ref, o_ref):
    h = h_ref[...]
    r = _rms(h, gp_ref[...]).astype(BF16)
    gate = jax.nn.sigmoid(jnp.dot(r, wpg_ref[...], preferred_element_type=F32))
    pe = jnp.dot(p_ref[...].astype(BF16), wpe_ref[...], preferred_element_type=F32)
    o_ref[...] = _rms(h + gate * pe, gf_ref[...])


def _ple_final(h, p2, gp, wpg, wpe, gf, bm=512):
    s, d = h.shape
    pd = p2.shape[1]
    full = lambda a: pl.BlockSpec(a.shape, lambda i: (0,) * a.ndim)
    return pl.pallas_call(
        _ple_kernel,
        grid=(s // bm,),
        in_specs=[pl.BlockSpec((bm, d), lambda i: (i, 0)),
                  pl.BlockSpec((bm, pd), lambda i: (i, 0)),
                  full(gp), full(wpg), full(wpe), full(gf)],
        out_specs=pl.BlockSpec((bm, d), lambda i: (i, 0)),
        out_shape=jax.ShapeDtypeStruct((s, d), F32),
        compiler_params=pltpu.CompilerParams(
            dimension_semantics=("arbitrary",), vmem_limit_bytes=VMEM_LIMIT),
        name="ple_final",
    )(h, p2, gp, wpg, wpe, gf)


def _prep_in_weights(w_in):
    d = w_in.shape[0]
    h, nope, rope = N_HEADS, HEAD_DIM, MLA_ROPE
    half = rope // 2
    q_cols = h * (nope + rope)
    c0 = q_cols
    c1 = c0 + KV_RANK
    c2 = c1 + rope
    mw = h * HEAD_DIM
    z = lambda n: jnp.zeros((d, n), w_in.dtype)

    def rope_slab(w):
        return jnp.concatenate([w[:, :half], z(half), w[:, half:], z(half)], axis=1)

    wq = w_in[:, :q_cols].reshape(d, h, nope + rope)
    wq_heads = [jnp.concatenate([wq[:, i, :nope], rope_slab(wq[:, i, nope:])], axis=1) for i in range(h)]
    w_main = jnp.concatenate(wq_heads + [w_in[:, c2:c2 + 2 * mw]], axis=1).astype(BF16)
    w_vt = w_in[:, c2 + 2 * mw:c2 + 3 * mw].T.astype(BF16)
    w_ckv = jnp.concatenate([w_in[:, c0:c1], rope_slab(w_in[:, c1:c2])], axis=1).astype(BF16)
    return w_main, w_vt, w_ckv


def kernel(x, p, positions, attn_norm, w_in, kv_norm, w_ukv, w_o, ffn_norm, w_gate, w_up, w_down,
           ple_norm, w_ple_gate, w_ple_proj, final_norm):
    b, s, d = x.shape
    assert b == 1 and p.shape[0] == 1 and s % 1024 == 0
    x2 = x.reshape(s, d)
    p2 = p.reshape(s, p.shape[-1])
    row = lambda v: v.reshape(1, -1).astype(F32)

    w_main, w_vt, w_ckv = _prep_in_weights(w_in[0])
    wukv = w_ukv[0].reshape(KV_RANK, N_HEADS, 2 * HEAD_DIM)
    w_uk = wukv[:, :, :HEAD_DIM].reshape(KV_RANK, -1).astype(BF16)
    w_uvt = wukv[:, :, HEAD_DIM:].reshape(KV_RANK, -1).T.astype(BF16)
    mla_out = N_HEADS * HEAD_DIM
    wo_a = w_o[0][:mla_out].astype(BF16)
    wo_b = w_o[0][mla_out:].astype(BF16)

    cos, sin = _rope_tables(positions)
    proj, kmean, vt_mb = _in_proj(x2, row(attn_norm[0]), w_main, w_vt, cos, sin)
    k_mla, vt_mla = _mla_kv(x2, row(attn_norm[0]), w_ckv, row(kv_norm[0]), w_uk, w_uvt, cos, sin)

    mw = N_HEADS * HEAD_DIM
    q_mla_w = N_HEADS * QK_WIDTH
    kmean = kmean.reshape(s // MOBA_BLOCK, mw)
    q_aug, k_aug = _moba_gate(proj, kmean, q_mla_w, q_mla_w + mw)

    out_mla = _flash(proj, k_mla, vt_mla)
    out_mb = _flash(q_aug, k_aug, vt_mb)

    h1 = _out_proj(x2, out_mla, out_mb, wo_a, wo_b)
    h2 = _ffn(h1, row(ffn_norm[0]), w_gate[0].astype(BF16), w_up[0].astype(BF16), w_down[0].astype(BF16))
    out = _ple_final(h2, p2, row(ple_norm[0]), w_ple_gate[0].astype(BF16), w_ple_proj[0].astype(BF16),
                     row(final_norm))
    return out.reshape(b, s, d)
```

```python
import functools

import jax
import jax.numpy as jnp
from jax import lax
from jax.experimental import pallas as pl
from jax.experimental.pallas import tpu as pltpu

F32 = jnp.float32
BF16 = jnp.bfloat16

EPS = 1e-6
NEG = -1e30
M_INIT = -3.0e38
ROPE_THETA = 10000.0

LANES = 128
HEAD_DIM = 128
MLA_ROPE = 64
KV_RANK = 512
N_HEADS = 8
MOBA_BLOCK = 256
MOBA_TOPK = 3
QK_WIDTH = 2 * LANES
VMEM_LIMIT = 56 * 1024 * 1024


def _rms(x, g):
    ms = jnp.mean(x * x, axis=-1, keepdims=True)
    return x * lax.rsqrt(ms + EPS) * g


def _rope(slab, cos, sin):
    return slab * cos + pltpu.roll(slab, LANES // 2, axis=1) * sin


def _rope_table_kernel(pos_ref, inv_ref, cos_ref, sin_ref):
    pos = pos_ref[...].astype(F32)
    lane = lax.broadcasted_iota(jnp.int32, (1, LANES), 1)
    sign = jnp.where(lane < LANES // 2, -1.0, 1.0).astype(F32)
    for t in range(2):
        ang = pos * inv_ref[t:t + 1, :]
        cos_ref[t] = jnp.cos(ang)
        sin_ref[t] = jnp.sin(ang) * sign


def _rope_tables(positions, bm=1024):
    s = positions.shape[-1]
    pos = positions.reshape(s, 1)
    half_mb = HEAD_DIM // 2
    half_ml = MLA_ROPE // 2
    inv_mb = ROPE_THETA ** (-(jnp.arange(half_mb, dtype=F32) * 2.0 / HEAD_DIM))
    inv_ml = ROPE_THETA ** (-(jnp.arange(half_ml, dtype=F32) * 2.0 / MLA_ROPE))
    inv = jnp.stack([jnp.tile(inv_mb, LANES // half_mb), jnp.tile(inv_ml, LANES // half_ml)])
    out = jax.ShapeDtypeStruct((2, s, LANES), F32)
    return pl.pallas_call(
        _rope_table_kernel,
        grid=(s // bm,),
        in_specs=[pl.BlockSpec((bm, 1), lambda i: (i, 0)),
                  pl.BlockSpec((2, LANES), lambda i: (0, 0))],
        out_specs=[pl.BlockSpec((2, bm, LANES), lambda i: (0, i, 0))] * 2,
        out_shape=[out, out],
        name="rope_tables",
    )(pos, inv)


def _proj_kernel(x_ref, g_ref, w_ref, wvt_ref, cos_ref, sin_ref, o_ref, km_ref, vt_ref, a_scr, *,
                 scale_mla, scale_mb):
    j = pl.program_id(1)
    bm = x_ref.shape[0]

    @pl.when(j == 0)
    def _():
        a_scr[...] = _rms(x_ref[...], g_ref[...]).astype(BF16)

    def heads():
        acc = jnp.dot(a_scr[...], w_ref[...], preferred_element_type=F32)
        return lambda h: acc[:, h * LANES:(h + 1) * LANES]

    @pl.when(j < 2)
    def _():
        head = heads()
        for h in range(4):
            o_ref[:, (2 * h) * LANES:(2 * h + 1) * LANES] = (head(2 * h) * scale_mla).astype(BF16)
            r = _rope(head(2 * h + 1), cos_ref[1], sin_ref[1])
            o_ref[:, (2 * h + 1) * LANES:(2 * h + 2) * LANES] = (r * scale_mla).astype(BF16)

    @pl.when(j == 2)
    def _():
        head = heads()
        for h in range(N_HEADS):
            r = _rope(head(h), cos_ref[0], sin_ref[0])
            o_ref[:, h * LANES:(h + 1) * LANES] = (r * scale_mb).astype(BF16)

    @pl.when(j == 3)
    def _():
        head = heads()
        nb = bm // MOBA_BLOCK
        for h in range(N_HEADS):
            r = _rope(head(h), cos_ref[0], sin_ref[0])
            o_ref[:, h * LANES:(h + 1) * LANES] = r.astype(BF16)
            km = r.reshape(nb, MOBA_BLOCK, LANES).sum(axis=1) * (1.0 / MOBA_BLOCK)
            km_ref[0, :, h * LANES:(h + 1) * LANES] = km

    @pl.when(j == 4)
    def _():
        vt = lax.dot_general(wvt_ref[...], a_scr[...], (((1,), (1,)), ((), ())), preferred_element_type=F32)
        vt_ref[...] = vt.astype(BF16)


def _in_proj(x2, g, w, wvt, cos, sin, bm=1024, bn=1024):
    s, d = x2.shape
    n = w.shape[1] + wvt.shape[0]
    nt = n // bn
    nb = bm // MOBA_BLOCK
    log2e = 1.4426950408889634
    kern = functools.partial(_proj_kernel, scale_mla=float((HEAD_DIM + MLA_ROPE) ** -0.5 * log2e),
                             scale_mb=float(HEAD_DIM ** -0.5 * log2e))
    return pl.pallas_call(
        kern,
        grid=(s // bm, nt),
        in_specs=[pl.BlockSpec((bm, d), lambda i, j: (i, 0)),
                  pl.BlockSpec((1, d), lambda i, j: (0, 0)),
                  pl.BlockSpec((d, bn), lambda i, j: (0, jnp.minimum(j, nt - 2))),
                  pl.BlockSpec((bn, d), lambda i, j: (0, 0)),
                  pl.BlockSpec((2, bm, LANES), lambda i, j: (0, i, 0)),
                  pl.BlockSpec((2, bm, LANES), lambda i, j: (0, i, 0))],
        out_specs=[pl.BlockSpec((bm, bn), lambda i, j: (i, jnp.minimum(j, nt - 2))),
                   pl.BlockSpec((1, nb, bn), lambda i, j: (i, 0, 0)),
                   pl.BlockSpec((bn, bm), lambda i, j: (0, i))],
        out_shape=[jax.ShapeDtypeStruct((s, n - bn), BF16),
                   jax.ShapeDtypeStruct((s // bm, nb, bn), F32),
                   jax.ShapeDtypeStruct((bn, s), BF16)],
        scratch_shapes=[pltpu.VMEM((bm, d), BF16)],
        compiler_params=pltpu.CompilerParams(
            dimension_semantics=("arbitrary", "arbitrary"), vmem_limit_bytes=VMEM_LIMIT),
        name="in_proj",
    )(x2, g, w, wvt, cos, sin)


def _mla_kv_kernel(x_ref, g_ref, wc_ref, kvg_ref, wuk_ref, wuvt_ref, cos_ref, sin_ref, k_ref, vt_ref):
    a = _rms(x_ref[...], g_ref[...]).astype(BF16)
    t = jnp.dot(a, wc_ref[...], preferred_element_type=F32)
    cn = _rms(t[:, :KV_RANK], kvg_ref[...]).astype(BF16)
    kn = jnp.dot(cn, wuk_ref[...], preferred_element_type=F32)
    vt = lax.dot_general(wuvt_ref[...], cn, (((1,), (1,)), ((), ())), preferred_element_type=F32)
    kpe = _rope(t[:, KV_RANK:], cos_ref[0], sin_ref[0]).astype(BF16)
    for h in range(N_HEADS):
        k_ref[:, (2 * h) * LANES:(2 * h + 1) * LANES] = kn[:, h * LANES:(h + 1) * LANES].astype(BF16)
        k_ref[:, (2 * h + 1) * LANES:(2 * h + 2) * LANES] = kpe
    vt_ref[...] = vt.astype(BF16)


def _mla_kv(x2, g, wc, kvg, wuk, wuv, cos, sin, bm=512):
    s, d = x2.shape
    nc = wc.shape[1]
    full = lambda a: pl.BlockSpec(a.shape, lambda i: (0,) * a.ndim)
    return pl.pallas_call(
        _mla_kv_kernel,
        grid=(s // bm,),
        in_specs=[pl.BlockSpec((bm, d), lambda i: (i, 0)), full(g), full(wc), full(kvg), full(wuk), full(wuv),
                  pl.BlockSpec((1, bm, LANES), lambda i: (1, i, 0)),
                  pl.BlockSpec((1, bm, LANES), lambda i: (1, i, 0))],
        out_specs=[pl.BlockSpec((bm, N_HEADS * QK_WIDTH), lambda i: (i, 0)),
                   pl.BlockSpec((N_HEADS * HEAD_DIM, bm), lambda i: (0, i))],
        out_shape=[jax.ShapeDtypeStruct((s, N_HEADS * QK_WIDTH), BF16),
                   jax.ShapeDtypeStruct((N_HEADS * HEAD_DIM, s), BF16)],
        compiler_params=pltpu.CompilerParams(
            dimension_semantics=("arbitrary",), vmem_limit_bytes=VMEM_LIMIT),
        name="mla_kv",
    )(x2, g, wc, kvg, wuk, wuv, cos, sin)


def _gate_kernel(q_ref, k_ref, km_ref, qa_ref, ka_ref):
    i = pl.program_id(0)
    bm = q_ref.shape[0]
    nblk = km_ref.shape[0]
    col = lax.broadcasted_iota(jnp.int32, (bm, LANES), 1)
    row = i * bm + lax.broadcasted_iota(jnp.int32, (bm, LANES), 0)
    blk = lax.shift_right_logical(row, 8)
    colf = col.astype(F32)
    past = col < blk
    own = jnp.where(col == blk, 1.0, 0.0)
    onehot = own.astype(BF16)
    zpad = jnp.zeros((LANES - nblk, LANES), F32)
    for h in range(N_HEADS):
        q = q_ref[:, h * LANES:(h + 1) * LANES]
        km = km_ref[:, h * LANES:(h + 1) * LANES]
        km_pad = jnp.concatenate([km, zpad], axis=0).astype(BF16)
        gate = lax.dot_general(q, km_pad, (((1,), (1,)), ((), ())), preferred_element_type=F32)
        g = jnp.where(past, gate, NEG)
        sel = own
        for t in range(MOBA_TOPK):
            m = jnp.max(g, axis=-1, keepdims=True)
            idx = jnp.min(jnp.where(g == m, colf, float(LANES)), axis=-1, keepdims=True)
            pick = colf == idx
            sel = jnp.where(pick, jnp.where(blk > t, 1.0, sel), sel)
            g = jnp.where(pick, -jnp.inf, g)
        qa_ref[:, (2 * h) * LANES:(2 * h + 1) * LANES] = q
        qa_ref[:, (2 * h + 1) * LANES:(2 * h + 2) * LANES] = jnp.where(sel > 0.5, 0.0, NEG).astype(BF16)
        ka_ref[:, (2 * h) * LANES:(2 * h + 1) * LANES] = k_ref[:, h * LANES:(h + 1) * LANES]
        ka_ref[:, (2 * h + 1) * LANES:(2 * h + 2) * LANES] = onehot


def _moba_gate(proj, kmean, q_col0, k_col0, bm=512):
    s = proj.shape[0]
    nblk, mw = kmean.shape
    qb, kb = q_col0 // mw, k_col0 // mw
    out = jax.ShapeDtypeStruct((s, N_HEADS * QK_WIDTH), BF16)
    return pl.pallas_call(
        _gate_kernel,
        grid=(s // bm,),
        in_specs=[pl.BlockSpec((bm, mw), lambda i: (i, qb)),
                  pl.BlockSpec((bm, mw), lambda i: (i, kb)),
                  pl.BlockSpec((nblk, mw), lambda i: (0, 0))],
        out_specs=[pl.BlockSpec((bm, N_HEADS * QK_WIDTH), lambda i: (i, 0))] * 2,
        out_shape=[out, out],
        compiler_params=pltpu.CompilerParams(dimension_semantics=("arbitrary",)),
        name="moba_gate",
    )(proj, proj, kmean)


def _flash_kernel(q_ref, k_ref, vt_ref, o_ref, qt_scr, s_scr, cm_scr, m_scr, l_scr, acc_scr, *, ng, tg, tk):
    qi = pl.program_id(1)
    tq = ng * tg
    r = tk // tg
    m_scr[...] = jnp.full(m_scr.shape, M_INIT, F32)
    l_scr[...] = jnp.zeros(l_scr.shape, F32)
    acc_scr[...] = jnp.zeros(acc_scr.shape, F32)
    qt_scr[...] = q_ref[...].astype(F32).T.astype(BF16)

    def start(buf, g, kj):
        k = k_ref[pl.ds(pl.multiple_of(kj * tk, tk), tk), :]
        st = jnp.dot(k, qt_scr[:, g * tg:(g + 1) * tg], preferred_element_type=F32)
        s_scr[buf] = st
        cm_scr[buf] = jnp.max(st, axis=0, keepdims=True)

    def finish(buf, g, kj, masked):
        st = s_scr[buf]
        if masked:
            key = kj * tk + lax.broadcasted_iota(jnp.int32, (tk, tg), 0)
            qry = qi * tq + g * tg + lax.broadcasted_iota(jnp.int32, (tk, tg), 1)
            st = jnp.where(key <= qry, st, NEG)
            cm = jnp.max(st, axis=0, keepdims=True)
        else:
            cm = cm_scr[buf]
        m_prev = m_scr[g]
        m_new = jnp.maximum(m_prev, cm)
        alpha = jnp.exp2(m_prev - m_new)
        p = jnp.exp2(st - m_new)
        l_scr[g] = alpha * l_scr[g] + jnp.sum(p, axis=0, keepdims=True)
        vt = vt_ref[:, pl.ds(pl.multiple_of(kj * tk, tk), tk)]
        acc_scr[g] = alpha * acc_scr[g] + jnp.dot(vt, p.astype(BF16), preferred_element_type=F32)
        m_scr[g] = m_new

    n_full = qi * (ng // r)
    tail = [(c, g, g < (c + 1) * r) for c in range(ng // r) for g in range(c * r, ng)]

    start(0, 0, 0)

    def chunk(kj):
        for g in range(ng):
            if g + 1 < ng:
                start((g + 1) % 2, g + 1, kj)
            else:
                start(0, 0, kj + 1)
            finish(g % 2, g, kj, False)

    unroll = 2 if (ng // r) % 2 == 0 else 1

    def body(t, c):
        for u in range(unroll):
            chunk(t * unroll + u)
        return c

    lax.fori_loop(0, n_full // unroll, body, 0)
    for i, (d, g, masked) in enumerate(tail):
        if i + 1 < len(tail):
            nd, ngp, _ = tail[i + 1]
            start((i + 1) % 2, ngp, n_full + nd)
        finish(i % 2, g, n_full + d, masked)
    for g in range(ng):
        o = acc_scr[g] / l_scr[g]
        o_ref[g * tg:(g + 1) * tg, :] = o.T.astype(o_ref.dtype)


def _flash(q, k, vt, q_col0=0, ng=4, tg=512, tk=512):
    s = q.shape[0]
    tq = ng * tg
    assert ng % 2 == 0 and tk % tg == 0 and tq % tk == 0
    qb = q_col0 // QK_WIDTH
    kern = functools.partial(_flash_kernel, ng=ng, tg=tg, tk=tk)
    return pl.pallas_call(
        kern,
        grid=(N_HEADS, s // tq),
        in_specs=[pl.BlockSpec((tq, QK_WIDTH), lambda h, i: (i, qb + h)),
                  pl.BlockSpec((s, QK_WIDTH), lambda h, i: (0, h)),
                  pl.BlockSpec((HEAD_DIM, s), lambda h, i: (h, 0))],
        out_specs=pl.BlockSpec((tq, HEAD_DIM), lambda h, i: (i, h)),
        out_shape=jax.ShapeDtypeStruct((s, N_HEADS * HEAD_DIM), BF16),
        scratch_shapes=[pltpu.VMEM((QK_WIDTH, tq), BF16),
                        pltpu.VMEM((2, tk, tg), F32), pltpu.VMEM((2, 1, tg), F32),
                        pltpu.VMEM((ng, 1, tg), F32), pltpu.VMEM((ng, 1, tg), F32),
                        pltpu.VMEM((ng, HEAD_DIM, tg), F32)],
        compiler_params=pltpu.CompilerParams(
            dimension_semantics=("arbitrary", "arbitrary"), vmem_limit_bytes=VMEM_LIMIT),
        name="flash_attn",
    )(q, k, vt)


def _wo_kernel(x_ref, a_ref, b_ref, wa_ref, wb_ref, o_ref):
    o_ref[...] = (x_ref[...]
                  + jnp.dot(a_ref[...], wa_ref[...], preferred_element_type=F32)
                  + jnp.dot(b_ref[...], wb_ref[...], preferred_element_type=F32))


def _out_proj(x2, oa, ob, wa, wb, bm=1024, bn=1024):
    s, d = x2.shape
    ka, kb = oa.shape[1], ob.shape[1]
    return pl.pallas_call(
        _wo_kernel,
        grid=(s // bm, d // bn),
        in_specs=[pl.BlockSpec((bm, bn), lambda i, j: (i, j)),
                  pl.BlockSpec((bm, ka), lambda i, j: (i, 0)),
                  pl.BlockSpec((bm, kb), lambda i, j: (i, 0)),
                  pl.BlockSpec((ka, bn), lambda i, j: (0, j)),
                  pl.BlockSpec((kb, bn), lambda i, j: (0, j))],
        out_specs=pl.BlockSpec((bm, bn), lambda i, j: (i, j)),
        out_shape=jax.ShapeDtypeStruct((s, d), F32),
        compiler_params=pltpu.CompilerParams(
            dimension_semantics=("arbitrary", "arbitrary"), vmem_limit_bytes=VMEM_LIMIT),
        name="out_proj",
    )(x2, oa, ob, wa, wb)


def _ffn_kernel(h_ref, g_ref, wg_ref, wu_ref, wd_ref, o_ref, f_scr):
    k = pl.program_id(1)

    @pl.when(k == 0)
    def _():
        h = h_ref[...]
        f_scr[...] = _rms(h, g_ref[...]).astype(BF16)
        o_ref[...] = h

    f = f_scr[...]
    gt = jnp.dot(f, wg_ref[...], preferred_element_type=F32)
    up = jnp.dot(f, wu_ref[...], preferred_element_type=F32)
    act = (gt * jax.nn.sigmoid(gt) * up).astype(BF16)
    o_ref[...] += jnp.dot(act, wd_ref[...], preferred_element_type=F32)


def _ffn(h, g, wg, wu, wd, bm=1024, bf=512):
    s, d = h.shape
    f = wg.shape[1]
    return pl.pallas_call(
        _ffn_kernel,
        grid=(s // bm, f // bf),
        in_specs=[pl.BlockSpec((bm, d), lambda i, k: (i, 0), pipeline_mode=pl.Buffered(1)),
                  pl.BlockSpec((1, d), lambda i, k: (0, 0)),
                  pl.BlockSpec((d, bf), lambda i, k: (0, k)),
                  pl.BlockSpec((d, bf), lambda i, k: (0, k)),
                  pl.BlockSpec((bf, d), lambda i, k: (k, 0))],
        out_specs=pl.BlockSpec((bm, d), lambda i, k: (i, 0)),
        out_shape=jax.ShapeDtypeStruct((s, d), F32),
        scratch_shapes=[pltpu.VMEM((bm, d), BF16)],
        compiler_params=pltpu.CompilerParams(
            dimension_semantics=("arbitrary", "arbitrary"), vmem_limit_bytes=VMEM_LIMIT),
        name="swiglu",
    )(h, g, wg, wu, wd)


def _ple_kernel(h_ref, p_ref, gp_ref, wpg_ref, wpe_ref, gf_ref, o_ref):
    h = h_ref[...]
    r = _rms(h, gp_ref[...]).astype(BF16)
    gate = jax.nn.sigmoid(jnp.dot(r, wpg_ref[...], preferred_element_type=F32))
    pe = jnp.dot(p_ref[...].astype(BF16), wpe_ref[...], preferred_element_type=F32)
    o_ref[...] = _rms(h + gate * pe, gf_ref[...])


def _ple_final(h, p2, gp, wpg, wpe, gf, bm=512):
    s, d = h.shape
    pd = p2.shape[1]
    full = lambda a: pl.BlockSpec(a.shape, lambda i: (0,) * a.ndim)
    return pl.pallas_call(
        _ple_kernel,
        grid=(s // bm,),
        in_specs=[pl.BlockSpec((bm, d), lambda i: (i, 0)),
                  pl.BlockSpec((bm, pd), lambda i: (i, 0)),
                  full(gp), full(wpg), full(wpe), full(gf)],
        out_specs=pl.BlockSpec((bm, d), lambda i: (i, 0)),
        out_shape=jax.ShapeDtypeStruct((s, d), F32),
        compiler_params=pltpu.CompilerParams(
            dimension_semantics=("arbitrary",), vmem_limit_bytes=VMEM_LIMIT),
        name="ple_final",
    )(h, p2, gp, wpg, wpe, gf)


def _prep_in_weights(w_in):
    d = w_in.shape[0]
    h, nope, rope = N_HEADS, HEAD_DIM, MLA_ROPE
    half = rope // 2
    q_cols = h * (nope + rope)
    c0 = q_cols
    c1 = c0 + KV_RANK
    c2 = c1 + rope
    mw = h * HEAD_DIM
    z = lambda n: jnp.zeros((d, n), w_in.dtype)

    def rope_slab(w):
        return jnp.concatenate([w[:, :half], z(half), w[:, half:], z(half)], axis=1)

    wq = w_in[:, :q_cols].reshape(d, h, nope + rope)
    wq_heads = [jnp.concatenate([wq[:, i, :nope], rope_slab(wq[:, i, nope:])], axis=1) for i in range(h)]
    w_main = jnp.concatenate(wq_heads + [w_in[:, c2:c2 + 2 * mw]], axis=1).astype(BF16)
    w_vt = w_in[:, c2 + 2 * mw:c2 + 3 * mw].T.astype(BF16)
    w_ckv = jnp.concatenate([w_in[:, c0:c1], rope_slab(w_in[:, c1:c2])], axis=1).astype(BF16)
    return w_main, w_vt, w_ckv


def kernel(x, p, positions, attn_norm, w_in, kv_norm, w_ukv, w_o, ffn_norm, w_gate, w_up, w_down,
           ple_norm, w_ple_gate, w_ple_proj, final_norm):
    b, s, d = x.shape
    assert b == 1 and p.shape[0] == 1 and s % 1024 == 0
    x2 = x.reshape(s, d)
    p2 = p.reshape(s, p.shape[-1])
    row = lambda v: v.reshape(1, -1).astype(F32)

    w_main, w_vt, w_ckv = _prep_in_weights(w_in[0])
    wukv = w_ukv[0].reshape(KV_RANK, N_HEADS, 2 * HEAD_DIM)
    w_uk = wukv[:, :, :HEAD_DIM].reshape(KV_RANK, -1).astype(BF16)
    w_uvt = wukv[:, :, HEAD_DIM:].reshape(KV_RANK, -1).T.astype(BF16)
    mla_out = N_HEADS * HEAD_DIM
    wo_a = w_o[0][:mla_out].astype(BF16)
    wo_b = w_o[0][mla_out:].astype(BF16)

    cos, sin = _rope_tables(positions)
    proj, kmean, vt_mb = _in_proj(x2, row(attn_norm[0]), w_main, w_vt, cos, sin)
    k_mla, vt_mla = _mla_kv(x2, row(attn_norm[0]), w_ckv, row(kv_norm[0]), w_uk, w_uvt, cos, sin)

    mw = N_HEADS * HEAD_DIM
    q_mla_w = N_HEADS * QK_WIDTH
    kmean = kmean.reshape(s // MOBA_BLOCK, mw)
    q_aug, k_aug = _moba_gate(proj, kmean, q_mla_w, q_mla_w + mw)

    out_mla = _flash(proj, k_mla, vt_mla)
    out_mb = _flash(q_aug, k_aug, vt_mb)

    h1 = _out_proj(x2, out_mla, out_mb, wo_a, wo_b)
    h2 = _ffn(h1, row(ffn_norm[0]), w_gate[0].astype(BF16), w_up[0].astype(BF16), w_down[0].astype(BF16))
    out = _ple_final(h2, p2, row(ple_norm[0]), w_ple_gate[0].astype(BF16), w_ple_proj[0].astype(BF16),
                     row(final_norm))
    return out.reshape(b, s, d)
```

```python
import functools

import jax
import jax.numpy as jnp
from jax import lax
from jax.experimental import pallas as pl
from jax.experimental.pallas import tpu as pltpu

F32 = jnp.float32
BF16 = jnp.bfloat16

EPS = 1e-6
NEG = -1e30
M_INIT = -3.0e38
ROPE_THETA = 10000.0

LANES = 128
HEAD_DIM = 128
MLA_ROPE = 64
KV_RANK = 512
N_HEADS = 8
MOBA_BLOCK = 256
MOBA_TOPK = 3
QK_WIDTH = 2 * LANES
ONES_ROWS = 16
VMEM_LIMIT = 56 * 1024 * 1024


def _rms(x, g):
    ms = jnp.mean(x * x, axis=-1, keepdims=True)
    return x * lax.rsqrt(ms + EPS) * g


def _rope(slab, cos, sin):
    return slab * cos + pltpu.roll(slab, LANES // 2, axis=1) * sin


def _rope_table_kernel(pos_ref, inv_ref, cos_ref, sin_ref):
    pos = pos_ref[...].astype(F32)
    ang = pos * inv_ref[...]
    c = jnp.cos(ang)
    s = jnp.sin(ang)
    c_sw = pltpu.roll(c, LANES // 2, axis=1)
    s_sw = pltpu.roll(s, LANES // 2, axis=1)
    low = lax.broadcasted_iota(jnp.int32, ang.shape, 1) < LANES // 2
    cos_ref[0] = jnp.where(low, c, c_sw)
    sin_ref[0] = jnp.where(low, -s, s_sw)
    cos_ref[1] = jnp.where(low, c_sw, c)
    sin_ref[1] = jnp.where(low, -s_sw, s)


def _rope_tables(positions, bm=1024):
    s = positions.shape[-1]
    pos = positions.reshape(s, 1)
    half_mb = HEAD_DIM // 2
    half_ml = MLA_ROPE // 2
    inv_mb = ROPE_THETA ** (-(jnp.arange(half_mb, dtype=F32) * 2.0 / HEAD_DIM))
    inv_ml = ROPE_THETA ** (-(jnp.arange(half_ml, dtype=F32) * 2.0 / MLA_ROPE))
    inv = jnp.concatenate([inv_mb, inv_ml, jnp.zeros((LANES - half_mb - half_ml,), F32)]).reshape(1, LANES)
    out = jax.ShapeDtypeStruct((2, s, LANES), F32)
    return pl.pallas_call(
        _rope_table_kernel,
        grid=(s // bm,),
        in_specs=[pl.BlockSpec((bm, 1), lambda i: (i, 0)),
                  pl.BlockSpec((1, LANES), lambda i: (0, 0))],
        out_specs=[pl.BlockSpec((2, bm, LANES), lambda i: (0, i, 0))] * 2,
        out_shape=[out, out],
        name="rope_tables",
    )(pos, inv)


def _proj_kernel(x_ref, g_ref, w_ref, wvt_ref, cos_ref, sin_ref, o_ref, km_ref, vt_ref, a_scr, *,
                 scale_mla, scale_mb):
    j = pl.program_id(1)
    bm = x_ref.shape[0]

    @pl.when(j == 0)
    def _():
        a_scr[...] = _rms(x_ref[...], g_ref[...]).astype(BF16)

    def heads():
        acc = jnp.dot(a_scr[...], w_ref[...], preferred_element_type=F32)
        return lambda h: acc[:, h * LANES:(h + 1) * LANES]

    @pl.when(j < 2)
    def _():
        head = heads()
        for h in range(4):
            o_ref[:, (2 * h) * LANES:(2 * h + 1) * LANES] = (head(2 * h) * scale_mla).astype(BF16)
            r = _rope(head(2 * h + 1), cos_ref[1], sin_ref[1])
            o_ref[:, (2 * h + 1) * LANES:(2 * h + 2) * LANES] = (r * scale_mla).astype(BF16)

    @pl.when(j == 2)
    def _():
        head = heads()
        for h in range(N_HEADS):
            r = _rope(head(h), cos_ref[0], sin_ref[0])
            o_ref[:, h * LANES:(h + 1) * LANES] = (r * scale_mb).astype(BF16)

    @pl.when(j == 3)
    def _():
        head = heads()
        nb = bm // MOBA_BLOCK
        for h in range(N_HEADS):
            r = _rope(head(h), cos_ref[0], sin_ref[0])
            o_ref[:, h * LANES:(h + 1) * LANES] = r.astype(BF16)
            km = r.reshape(nb, MOBA_BLOCK, LANES).sum(axis=1) * (1.0 / MOBA_BLOCK)
            km_ref[0, :, h * LANES:(h + 1) * LANES] = km

    @pl.when(j == 4)
    def _():
        vt = lax.dot_general(wvt_ref[...], a_scr[...], (((1,), (1,)), ((), ())), preferred_element_type=F32)
        vt_ref[...] = vt.astype(BF16)


def _in_proj(x2, g, w, wvt, cos, sin, bm=1024, bn=1024):
    s, d = x2.shape
    n = w.shape[1] + wvt.shape[0]
    nt = n // bn
    nb = bm // MOBA_BLOCK
    log2e = 1.4426950408889634
    kern = functools.partial(_proj_kernel, scale_mla=float((HEAD_DIM + MLA_ROPE) ** -0.5 * log2e),
                             scale_mb=float(HEAD_DIM ** -0.5 * log2e))
    return pl.pallas_call(
        kern,
        grid=(s // bm, nt),
        in_specs=[pl.BlockSpec((bm, d), lambda i, j: (i, 0)),
                  pl.BlockSpec((1, d), lambda i, j: (0, 0)),
                  pl.BlockSpec((d, bn), lambda i, j: (0, jnp.minimum(j, nt - 2))),
                  pl.BlockSpec((bn, d), lambda i, j: (0, 0)),
                  pl.BlockSpec((2, bm, LANES), lambda i, j: (0, i, 0)),
                  pl.BlockSpec((2, bm, LANES), lambda i, j: (0, i, 0))],
        out_specs=[pl.BlockSpec((bm, bn), lambda i, j: (i, jnp.minimum(j, nt - 2))),
                   pl.BlockSpec((1, nb, bn), lambda i, j: (i, 0, 0)),
                   pl.BlockSpec((bn, bm), lambda i, j: (0, i))],
        out_shape=[jax.ShapeDtypeStruct((s, n - bn), BF16),
                   jax.ShapeDtypeStruct((s // bm, nb, bn), F32),
                   jax.ShapeDtypeStruct((bn, s), BF16)],
        scratch_shapes=[pltpu.VMEM((bm, d), BF16)],
        compiler_params=pltpu.CompilerParams(
            dimension_semantics=("arbitrary", "arbitrary"), vmem_limit_bytes=VMEM_LIMIT),
        name="in_proj",
    )(x2, g, w, wvt, cos, sin)


def _mla_kv_kernel(x_ref, g_ref, wc_ref, kvg_ref, wuk_ref, wuvt_ref, cos_ref, sin_ref, k_ref, vt_ref):
    a = _rms(x_ref[...], g_ref[...]).astype(BF16)
    t = jnp.dot(a, wc_ref[...], preferred_element_type=F32)
    cn = _rms(t[:, :KV_RANK], kvg_ref[...]).astype(BF16)
    kn = jnp.dot(cn, wuk_ref[...], preferred_element_type=F32)
    vt = lax.dot_general(wuvt_ref[...], cn, (((1,), (1,)), ((), ())), preferred_element_type=F32)
    kpe = _rope(t[:, KV_RANK:], cos_ref[0], sin_ref[0]).astype(BF16)
    for h in range(N_HEADS):
        k_ref[:, (2 * h) * LANES:(2 * h + 1) * LANES] = kn[:, h * LANES:(h + 1) * LANES].astype(BF16)
        k_ref[:, (2 * h + 1) * LANES:(2 * h + 2) * LANES] = kpe
    vt_ref[...] = vt.astype(BF16)


def _mla_kv(x2, g, wc, kvg, wuk, wuv, cos, sin, bm=512):
    s, d = x2.shape
    nc = wc.shape[1]
    full = lambda a: pl.BlockSpec(a.shape, lambda i: (0,) * a.ndim)
    return pl.pallas_call(
        _mla_kv_kernel,
        grid=(s // bm,),
        in_specs=[pl.BlockSpec((bm, d), lambda i: (i, 0)), full(g), full(wc), full(kvg), full(wuk), full(wuv),
                  pl.BlockSpec((1, bm, LANES), lambda i: (1, i, 0)),
                  pl.BlockSpec((1, bm, LANES), lambda i: (1, i, 0))],
        out_specs=[pl.BlockSpec((bm, N_HEADS * QK_WIDTH), lambda i: (i, 0)),
                   pl.BlockSpec((N_HEADS * HEAD_DIM, bm), lambda i: (0, i))],
        out_shape=[jax.ShapeDtypeStruct((s, N_HEADS * QK_WIDTH), BF16),
                   jax.ShapeDtypeStruct((N_HEADS * HEAD_DIM, s), BF16)],
        compiler_params=pltpu.CompilerParams(
            dimension_semantics=("arbitrary",), vmem_limit_bytes=VMEM_LIMIT),
        name="mla_kv",
    )(x2, g, wc, kvg, wuk, wuv, cos, sin)


def _gate_kernel(q_ref, k_ref, km_ref, qa_ref, ka_ref):
    i = pl.program_id(0)
    bm = q_ref.shape[0]
    nblk = km_ref.shape[0]
    col = lax.broadcasted_iota(jnp.int32, (bm, LANES), 1)
    row = i * bm + lax.broadcasted_iota(jnp.int32, (bm, LANES), 0)
    blk = lax.shift_right_logical(row, 8)
    colf = col.astype(F32)
    past = col < blk
    own = jnp.where(col == blk, 1.0, 0.0)
    onehot = own.astype(BF16)
    zpad = jnp.zeros((LANES - nblk, LANES), F32)
    for h in range(N_HEADS):
        q = q_ref[:, h * LANES:(h + 1) * LANES]
        km = km_ref[:, h * LANES:(h + 1) * LANES]
        km_pad = jnp.concatenate([km, zpad], axis=0).astype(BF16)
        gate = lax.dot_general(q, km_pad, (((1,), (1,)), ((), ())), preferred_element_type=F32)
        g = jnp.where(past, gate, NEG)
        sel = own
        for t in range(MOBA_TOPK):
            m = jnp.max(g, axis=-1, keepdims=True)
            idx = jnp.min(jnp.where(g == m, colf, float(LANES)), axis=-1, keepdims=True)
            pick = colf == idx
            sel = jnp.where(pick, jnp.where(blk > t, 1.0, sel), sel)
            g = jnp.where(pick, -jnp.inf, g)
        qa_ref[:, (2 * h) * LANES:(2 * h + 1) * LANES] = q
        qa_ref[:, (2 * h + 1) * LANES:(2 * h + 2) * LANES] = jnp.where(sel > 0.5, 0.0, NEG).astype(BF16)
        ka_ref[:, (2 * h) * LANES:(2 * h + 1) * LANES] = k_ref[:, h * LANES:(h + 1) * LANES]
        ka_ref[:, (2 * h + 1) * LANES:(2 * h + 2) * LANES] = onehot


def _moba_gate(proj, kmean, q_col0, k_col0, bm=512):
    s = proj.shape[0]
    nblk, mw = kmean.shape
    qb, kb = q_col0 // mw, k_col0 // mw
    out = jax.ShapeDtypeStruct((s, N_HEADS * QK_WIDTH), BF16)
    return pl.pallas_call(
        _gate_kernel,
        grid=(s // bm,),
        in_specs=[pl.BlockSpec((bm, mw), lambda i: (i, qb)),
                  pl.BlockSpec((bm, mw), lambda i: (i, kb)),
                  pl.BlockSpec((nblk, mw), lambda i: (0, 0))],
        out_specs=[pl.BlockSpec((bm, N_HEADS * QK_WIDTH), lambda i: (i, 0))] * 2,
        out_shape=[out, out],
        compiler_params=pltpu.CompilerParams(dimension_semantics=("arbitrary",)),
        name="moba_gate",
    )(proj, proj, kmean)


def _flash_kernel(q_ref, k_ref, vt_ref, o_ref, qt_scr, s_scr, cm_scr, m_scr, l_scr, acc_scr, *, ng, tg, tk):
    qi = pl.program_id(1)
    tq = ng * tg
    r = tk // tg
    m_scr[...] = jnp.full(m_scr.shape, M_INIT, F32)
    l_scr[...] = jnp.zeros(l_scr.shape, F32)
    acc_scr[...] = jnp.zeros(acc_scr.shape, F32)
    qt_scr[...] = q_ref[...].astype(F32).T.astype(BF16)

    def start(buf, g, kj):
        k = k_ref[pl.ds(pl.multiple_of(kj * tk, tk), tk), :]
        st = jnp.dot(k, qt_scr[:, g * tg:(g + 1) * tg], preferred_element_type=F32)
        s_scr[buf] = st
        cm_scr[buf] = jnp.max(st, axis=0, keepdims=True)

    def finish(buf, g, kj, masked):
        st = s_scr[buf]
        if masked:
            key = kj * tk + lax.broadcasted_iota(jnp.int32, (tk, tg), 0)
            qry = qi * tq + g * tg + lax.broadcasted_iota(jnp.int32, (tk, tg), 1)
            st = jnp.where(key <= qry, st, NEG)
            cm = jnp.max(st, axis=0, keepdims=True)
        else:
            cm = cm_scr[buf]
        m_prev = m_scr[g]
        m_new = jnp.maximum(m_prev, cm)
        alpha = jnp.exp2(m_prev - m_new)
        p = jnp.exp2(st - m_new).astype(BF16)
        vt = vt_ref[:, pl.ds(pl.multiple_of(kj * tk, tk), tk)]
        vt1 = jnp.concatenate([vt, jnp.ones((ONES_ROWS, tk), BF16)], axis=0)
        pv = jnp.dot(vt1, p, preferred_element_type=F32)
        l_scr[g] = alpha * l_scr[g] + pv[HEAD_DIM:HEAD_DIM + 1, :]
        acc_scr[g] = alpha * acc_scr[g] + pv[:HEAD_DIM, :]
        m_scr[g] = m_new

    n_full = qi * (ng // r)
    tail = [(c, g, g < (c + 1) * r) for c in range(ng // r) for g in range(c * r, ng)]

    start(0, 0, 0)

    def chunk(kj):
        for g in range(ng):
            if g + 1 < ng:
                start((g + 1) % 2, g + 1, kj)
            else:
                start(0, 0, kj + 1)
            finish(g % 2, g, kj, False)

    unroll = 2 if (ng // r) % 2 == 0 else 1

    def body(t, c):
        for u in range(unroll):
            chunk(t * unroll + u)
        return c

    lax.fori_loop(0, n_full // unroll, body, 0)
    for i, (d, g, masked) in enumerate(tail):
        if i + 1 < len(tail):
            nd, ngp, _ = tail[i + 1]
            start((i + 1) % 2, ngp, n_full + nd)
        finish(i % 2, g, n_full + d, masked)
    for g in range(ng):
        o = acc_scr[g] / l_scr[g]
        o_ref[g * tg:(g + 1) * tg, :] = o.T.astype(o_ref.dtype)


def _flash(q, k, vt, q_col0=0, ng=4, tg=512, tk=512):
    s = q.shape[0]
    tq = ng * tg
    assert ng % 2 == 0 and tk % tg == 0 and tq % tk == 0
    qb = q_col0 // QK_WIDTH
    kern = functools.partial(_flash_kernel, ng=ng, tg=tg, tk=tk)
    return pl.pallas_call(
        kern,
        grid=(N_HEADS, s // tq),
        in_specs=[pl.BlockSpec((tq, QK_WIDTH), lambda h, i: (i, qb + h)),
                  pl.BlockSpec((s, QK_WIDTH), lambda h, i: (0, h)),
                  pl.BlockSpec((HEAD_DIM, s), lambda h, i: (h, 0))],
        out_specs=pl.BlockSpec((tq, HEAD_DIM), lambda h, i: (i, h)),
        out_shape=jax.ShapeDtypeStruct((s, N_HEADS * HEAD_DIM), BF16),
        scratch_shapes=[pltpu.VMEM((QK_WIDTH, tq), BF16),
                        pltpu.VMEM((2, tk, tg), F32), pltpu.VMEM((2, 1, tg), F32),
                        pltpu.VMEM((ng, 1, tg), F32), pltpu.VMEM((ng, 1, tg), F32),
                        pltpu.VMEM((ng, HEAD_DIM, tg), F32)],
        compiler_params=pltpu.CompilerParams(
            dimension_semantics=("arbitrary", "arbitrary"), vmem_limit_bytes=VMEM_LIMIT),
        name="flash_attn",
    )(q, k, vt)


def _wo_kernel(x_ref, a_ref, b_ref, wa_ref, wb_ref, o_ref):
    o_ref[...] = (x_ref[...]
                  + jnp.dot(a_ref[...], wa_ref[...], preferred_element_type=F32)
                  + jnp.dot(b_ref[...], wb_ref[...], preferred_element_type=F32))


def _out_proj(x2, oa, ob, wa, wb, bm=1024, bn=1024):
    s, d = x2.shape
    ka, kb = oa.shape[1], ob.shape[1]
    return pl.pallas_call(
        _wo_kernel,
        grid=(s // bm, d // bn),
        in_specs=[pl.BlockSpec((bm, bn), lambda i, j: (i, j)),
                  pl.BlockSpec((bm, ka), lambda i, j: (i, 0)),
                  pl.BlockSpec((bm, kb), lambda i, j: (i, 0)),
                  pl.BlockSpec((ka, bn), lambda i, j: (0, j)),
                  pl.BlockSpec((kb, bn), lambda i, j: (0, j))],
        out_specs=pl.BlockSpec((bm, bn), lambda i, j: (i, j)),
        out_shape=jax.ShapeDtypeStruct((s, d), F32),
        compiler_params=pltpu.CompilerParams(
            dimension_semantics=("arbitrary", "arbitrary"), vmem_limit_bytes=VMEM_LIMIT),
        name="out_proj",
    )(x2, oa, ob, wa, wb)


def _ffn_kernel(h_ref, g_ref, wg_ref, wu_ref, wd_ref, o_ref, f_scr):
    k = pl.program_id(1)

    @pl.when(k == 0)
    def _():
        h = h_ref[...]
        f_scr[...] = _rms(h, g_ref[...]).astype(BF16)
        o_ref[...] = h

    f = f_scr[...]
    gt = jnp.dot(f, wg_ref[...], preferred_element_type=F32)
    up = jnp.dot(f, wu_ref[...], preferred_element_type=F32)
    act = (gt * jax.nn.sigmoid(gt) * up).astype(BF16)
    o_ref[...] += jnp.dot(act, wd_ref[...], preferred_element_type=F32)


def _ffn(h, g, wg, wu, wd, bm=512, bf=512):
    s, d = h.shape
    f = wg.shape[1]
    return pl.pallas_call(
        _ffn_kernel,
        grid=(s // bm, f // bf),
        in_specs=[pl.BlockSpec((bm, d), lambda i, k: (i, 0)),
                  pl.BlockSpec((1, d), lambda i, k: (0, 0)),
                  pl.BlockSpec((d, bf), lambda i, k: (0, k)),
                  pl.BlockSpec((d, bf), lambda i, k: (0, k)),
                  pl.BlockSpec((bf, d), lambda i, k: (k, 0))],
        out_specs=pl.BlockSpec((bm, d), lambda i, k: (i, 0)),
        out_shape=jax.ShapeDtypeStruct((s, d), F32),
        scratch_shapes=[pltpu.VMEM((bm, d), BF16)],
        compiler_params=pltpu.CompilerParams(
            dimension_semantics=("arbitrary", "arbitrary"), vmem_limit_bytes=VMEM_LIMIT),
        name="swiglu",
    )(h, g, wg, wu, wd)


def _ple_kernel(h_ref, p_ref, gp_ref, wpg_ref, wpe_ref, gf_ref, o_ref):
    h = h_ref[...]
    r = _rms(h, gp_ref[...]).astype(BF16)
    gate = jax.nn.sigmoid(jnp.dot(r, wpg_ref[...], preferred_element_type=F32))
    pe = jnp.dot(p_ref[...].astype(BF16), wpe_ref[...], preferred_element_type=F32)
    o_ref[...] = _rms(h + gate * pe, gf_ref[...])


def _ple_final(h, p2, gp, wpg, wpe, gf, bm=512):
    s, d = h.shape
    pd = p2.shape[1]
    full = lambda a: pl.BlockSpec(a.shape, lambda i: (0,) * a.ndim)
    return pl.pallas_call(
        _ple_kernel,
        grid=(s // bm,),
        in_specs=[pl.BlockSpec((bm, d), lambda i: (i, 0)),
                  pl.BlockSpec((bm, pd), lambda i: (i, 0)),
                  full(gp), full(wpg), full(wpe), full(gf)],
        out_specs=pl.BlockSpec((bm, d), lambda i: (i, 0)),
        out_shape=jax.ShapeDtypeStruct((s, d), F32),
        compiler_params=pltpu.CompilerParams(
            dimension_semantics=("arbitrary",), vmem_limit_bytes=VMEM_LIMIT),
        name="ple_final",
    )(h, p2, gp, wpg, wpe, gf)


def _prep_in_weights(w_in):
    d = w_in.shape[0]
    h, nope, rope = N_HEADS, HEAD_DIM, MLA_ROPE
    half = rope // 2
    q_cols = h * (nope + rope)
    c0 = q_cols
    c1 = c0 + KV_RANK
    c2 = c1 + rope
    mw = h * HEAD_DIM
    w_in = w_in.astype(BF16)

    def rope_slab(w):
        w2 = w.reshape(w.shape[:-1] + (2, half))
        w2 = jnp.pad(w2, [(0, 0)] * (w2.ndim - 1) + [(0, half)])
        return w2.reshape(w.shape[:-1] + (4 * half,))

    wq = w_in[:, :q_cols].reshape(d, h, nope + rope)
    wq = jnp.concatenate([wq[:, :, :nope], rope_slab(wq[:, :, nope:])], axis=2).reshape(d, h * QK_WIDTH)
    w_main = jnp.concatenate([wq, w_in[:, c2:c2 + 2 * mw]], axis=1)
    w_vt = w_in[:, c2 + 2 * mw:c2 + 3 * mw].T
    w_ckv = jnp.concatenate([w_in[:, c0:c1], rope_slab(w_in[:, c1:c2])], axis=1)
    return w_main, w_vt, w_ckv


def kernel(x, p, positions, attn_norm, w_in, kv_norm, w_ukv, w_o, ffn_norm, w_gate, w_up, w_down,
           ple_norm, w_ple_gate, w_ple_proj, final_norm):
    b, s, d = x.shape
    assert b == 1 and p.shape[0] == 1 and s % 1024 == 0
    x2 = x.reshape(s, d)
    p2 = p.reshape(s, p.shape[-1])
    row = lambda v: v.reshape(1, -1).astype(F32)

    w_main, w_vt, w_ckv = _prep_in_weights(w_in[0])
    wukv = w_ukv[0].reshape(KV_RANK, N_HEADS, 2 * HEAD_DIM)
    w_uk = wukv[:, :, :HEAD_DIM].reshape(KV_RANK, -1).astype(BF16)
    w_uvt = wukv[:, :, HEAD_DIM:].reshape(KV_RANK, -1).T.astype(BF16)
    mla_out = N_HEADS * HEAD_DIM
    wo_a = w_o[0][:mla_out].astype(BF16)
    wo_b = w_o[0][mla_out:].astype(BF16)

    cos, sin = _rope_tables(positions)
    proj, kmean, vt_mb = _in_proj(x2, row(attn_norm[0]), w_main, w_vt, cos, sin)
    k_mla, vt_mla = _mla_kv(x2, row(attn_norm[0]), w_ckv, row(kv_norm[0]), w_uk, w_uvt, cos, sin)

    mw = N_HEADS * HEAD_DIM
    q_mla_w = N_HEADS * QK_WIDTH
    kmean = kmean.reshape(s // MOBA_BLOCK, mw)
    q_aug, k_aug = _moba_gate(proj, kmean, q_mla_w, q_mla_w + mw)

    out_mla = _flash(proj, k_mla, vt_mla)
    out_mb = _flash(q_aug, k_aug, vt_mb)

    h1 = _out_proj(x2, out_mla, out_mb, wo_a, wo_b)
    h2 = _ffn(h1, row(ffn_norm[0]), w_gate[0].astype(BF16), w_up[0].astype(BF16), w_down[0].astype(BF16))
    out = _ple_final(h2, p2, row(ple_norm[0]), w_ple_gate[0].astype(BF16), w_ple_proj[0].astype(BF16),
                     row(final_norm))
    return out.reshape(b, s, d)
```

```python
import functools

import jax
import jax.numpy as jnp
from jax import lax
from jax.experimental import pallas as pl
from jax.experimental.pallas import tpu as pltpu

F32 = jnp.float32
BF16 = jnp.bfloat16

EPS = 1e-6
NEG = -1e30
M_INIT = -3.0e38
ROPE_THETA = 10000.0
LOG2E = 1.4426950408889634

LANES = 128
BF16_ROWS = 16
HEAD_DIM = 128
MLA_ROPE = 64
KV_RANK = 512
N_HEADS = 8
MOBA_BLOCK = 256
MOBA_TOPK = 3
QK_WIDTH = 2 * LANES
ONES_ROWS = BF16_ROWS
VMEM_LIMIT = 56 * 1024 * 1024

_NT = (((1,), (1,)), ((), ()))
_TN_T = (((0,), (1,)), ((), ()))


def _rms(x, g):
    ms = jnp.mean(x * x, axis=-1, keepdims=True)
    return x * lax.rsqrt(ms + EPS) * g


def _rope(slab, cos, sin):
    return slab * cos + pltpu.roll(slab, LANES // 2, axis=1) * sin


def _rope_t(x1, x2, c, s):
    return x1 * c - x2 * s, x2 * c + x1 * s


def _rope_table_kernel(pos_ref, inv_ref, ct_ref, st_ref, cos_ref, sin_ref):
    ang = inv_ref[...] * pos_ref[...].astype(F32)
    ct = jnp.cos(ang)
    st = jnp.sin(ang)
    ct_ref[...] = ct
    st_ref[...] = st
    c = ct.T
    s = st.T
    half, quart = LANES // 2, LANES // 4
    lane = lax.broadcasted_iota(jnp.int32, c.shape, 1)
    low = lane < half
    c_sw = pltpu.roll(c, half, axis=1)
    s_sw = pltpu.roll(s, half, axis=1)
    cos_ref[0] = jnp.where(low, c, c_sw)
    sin_ref[0] = jnp.where(low, -s, s_sw)
    c_q = pltpu.roll(c_sw, quart, axis=1)
    s_q = pltpu.roll(s_sw, quart, axis=1)
    first = lane < quart
    cos_ref[1] = jnp.where(first, c_sw, c_q)
    sin_ref[1] = jnp.where(first, -s_sw, s_q)


def _rope_tables(positions, bm=1024):
    s = positions.shape[-1]
    pos = positions.reshape(1, s)
    half_mb = HEAD_DIM // 2
    half_ml = MLA_ROPE // 2
    inv_mb = ROPE_THETA ** (-(jnp.arange(half_mb, dtype=F32) * 2.0 / HEAD_DIM))
    inv_ml = ROPE_THETA ** (-(jnp.arange(half_ml, dtype=F32) * 2.0 / MLA_ROPE))
    inv = jnp.concatenate([inv_mb, inv_ml, jnp.zeros((LANES - half_mb - half_ml,), F32)]).reshape(LANES, 1)
    t_out = jax.ShapeDtypeStruct((LANES, s), F32)
    r_out = jax.ShapeDtypeStruct((2, s, LANES), F32)
    return pl.pallas_call(
        _rope_table_kernel,
        grid=(s // bm,),
        in_specs=[pl.BlockSpec((1, bm), lambda i: (0, i)),
                  pl.BlockSpec((LANES, 1), lambda i: (0, 0))],
        out_specs=[pl.BlockSpec((LANES, bm), lambda i: (0, i))] * 2
                  + [pl.BlockSpec((2, bm, LANES), lambda i: (0, i, 0))] * 2,
        out_shape=[t_out, t_out, r_out, r_out],
        name="rope_tables",
    )(pos, inv)


def _proj_kernel(x_ref, g_ref, wq_ref, wm_ref, ct_ref, st_ref, cos_ref, sin_ref,
                 qtl_ref, qtm_ref, k_ref, km_ref, vt_ref, a_scr, *, scale_mla, scale_mb):
    j = pl.program_id(1)
    bm = x_ref.shape[0]
    half_mb, half_ml = HEAD_DIM // 2, MLA_ROPE // 2
    hq = HEAD_DIM + MLA_ROPE

    @pl.when(j == 0)
    def _():
        a_scr[...] = _rms(x_ref[...], g_ref[...]).astype(BF16)

    @pl.when(j < 2)
    def _():
        t = lax.dot_general(wq_ref[...], a_scr[...], _TN_T, preferred_element_type=F32)
        c = ct_ref[half_mb:half_mb + half_ml, :]
        s = st_ref[half_mb:half_mb + half_ml, :]
        for h in range(wq_ref.shape[1] // hq):
            r0, o0 = h * hq, h * QK_WIDTH
            qtl_ref[o0:o0 + HEAD_DIM, :] = (t[r0:r0 + HEAD_DIM, :] * scale_mla).astype(BF16)
            x1 = t[r0 + HEAD_DIM:r0 + HEAD_DIM + half_ml, :]
            x2 = t[r0 + HEAD_DIM + half_ml:r0 + hq, :]
            o1, o2 = _rope_t(x1, x2, c, s)
            qtl_ref[o0 + HEAD_DIM:o0 + HEAD_DIM + half_ml, :] = (o1 * scale_mla).astype(BF16)
            qtl_ref[o0 + HEAD_DIM + half_ml:o0 + hq, :] = (o2 * scale_mla).astype(BF16)
            qtl_ref[o0 + hq:o0 + QK_WIDTH, :] = jnp.zeros((QK_WIDTH - hq, bm), BF16)

    @pl.when(j == 2)
    def _():
        t = lax.dot_general(wm_ref[...], a_scr[...], _TN_T, preferred_element_type=F32)
        c = ct_ref[0:half_mb, :]
        s = st_ref[0:half_mb, :]
        for h in range(N_HEADS):
            r0 = h * HEAD_DIM
            o1, o2 = _rope_t(t[r0:r0 + half_mb, :], t[r0 + half_mb:r0 + HEAD_DIM, :], c, s)
            qtm_ref[r0:r0 + half_mb, :] = (o1 * scale_mb).astype(BF16)
            qtm_ref[r0 + half_mb:r0 + HEAD_DIM, :] = (o2 * scale_mb).astype(BF16)

    @pl.when(j == 3)
    def _():
        acc = jnp.dot(a_scr[...], wm_ref[...], preferred_element_type=F32)
        nb = bm // MOBA_BLOCK
        for h in range(N_HEADS):
            r = _rope(acc[:, h * LANES:(h + 1) * LANES], cos_ref[0], sin_ref[0])
            k_ref[:, h * LANES:(h + 1) * LANES] = r.astype(BF16)
            km = r.reshape(nb, MOBA_BLOCK, LANES).sum(axis=1) * (1.0 / MOBA_BLOCK)
            km_ref[0, :, h * LANES:(h + 1) * LANES] = km

    @pl.when(j == 4)
    def _():
        vt = lax.dot_general(wm_ref[...], a_scr[...], _TN_T, preferred_element_type=F32)
        vt_ref[...] = vt.astype(BF16)


def _in_proj(x2, g, wq, wm, ct, st, cos, sin, bm=512):
    s, d = x2.shape
    mw = N_HEADS * HEAD_DIM
    hq = HEAD_DIM + MLA_ROPE
    nq = wq.shape[1] // 2
    assert wq.shape[1] == N_HEADS * hq and wm.shape[1] == 3 * mw
    nb = bm // MOBA_BLOCK
    kern = functools.partial(_proj_kernel, scale_mla=float(hq ** -0.5 * LOG2E),
                             scale_mb=float(HEAD_DIM ** -0.5 * LOG2E))
    qw = N_HEADS * QK_WIDTH
    return pl.pallas_call(
        kern,
        grid=(s // bm, 5),
        in_specs=[pl.BlockSpec((bm, d), lambda i, j: (i, 0)),
                  pl.BlockSpec((1, d), lambda i, j: (0, 0)),
                  pl.BlockSpec((d, nq), lambda i, j: (0, jnp.minimum(j, 1))),
                  pl.BlockSpec((d, mw), lambda i, j: (0, jnp.clip(j - 2, 0, 2))),
                  pl.BlockSpec((LANES, bm), lambda i, j: (0, i)),
                  pl.BlockSpec((LANES, bm), lambda i, j: (0, i)),
                  pl.BlockSpec((2, bm, LANES), lambda i, j: (0, i, 0)),
                  pl.BlockSpec((2, bm, LANES), lambda i, j: (0, i, 0))],
        out_specs=[pl.BlockSpec((qw // 2, bm), lambda i, j: (jnp.minimum(j, 1), i)),
                   pl.BlockSpec((mw, bm), lambda i, j: (0, i)),
                   pl.BlockSpec((bm, mw), lambda i, j: (i, 0)),
                   pl.BlockSpec((1, nb, mw), lambda i, j: (i, 0, 0)),
                   pl.BlockSpec((mw, bm), lambda i, j: (0, i))],
        out_shape=[jax.ShapeDtypeStruct((qw, s), BF16),
                   jax.ShapeDtypeStruct((mw, s), BF16),
                   jax.ShapeDtypeStruct((s, mw), BF16),
                   jax.ShapeDtypeStruct((s // bm, nb, mw), F32),
                   jax.ShapeDtypeStruct((mw, s), BF16)],
        scratch_shapes=[pltpu.VMEM((bm, d), BF16)],
        compiler_params=pltpu.CompilerParams(
            dimension_semantics=("arbitrary", "arbitrary"), vmem_limit_bytes=VMEM_LIMIT),
        name="in_proj",
    )(x2, g, wq, wm, ct, st, cos, sin)


def _mla_kv_kernel(x_ref, g_ref, wc_ref, kvg_ref, wuk_ref, wuv_ref, cos_ref, sin_ref, k_ref, vt_ref):
    a = _rms(x_ref[...], g_ref[...]).astype(BF16)
    t = jnp.dot(a, wc_ref[...], preferred_element_type=F32)
    cn = _rms(t[:, :KV_RANK], kvg_ref[...]).astype(BF16)
    kn = jnp.dot(cn, wuk_ref[...], preferred_element_type=F32)
    vt = lax.dot_general(wuv_ref[...], cn, _TN_T, preferred_element_type=F32)
    slab = t[:, KV_RANK:]
    quart = LANES // 4
    lane = lax.broadcasted_iota(jnp.int32, slab.shape, 1)
    partner = jnp.where(lane < quart, pltpu.roll(slab, LANES - quart, axis=1), pltpu.roll(slab, quart, axis=1))
    kpe = (slab * cos_ref[0] + partner * sin_ref[0]).astype(BF16)
    for h in range(N_HEADS):
        k_ref[:, (2 * h) * LANES:(2 * h + 1) * LANES] = kn[:, h * LANES:(h + 1) * LANES].astype(BF16)
        k_ref[:, (2 * h + 1) * LANES:(2 * h + 2) * LANES] = kpe
    vt_ref[...] = vt.astype(BF16)


def _mla_kv(x2, g, wc, kvg, wuk, wuv, cos, sin, bm=512):
    s, d = x2.shape
    full = lambda a: pl.BlockSpec(a.shape, lambda i: (0,) * a.ndim)
    return pl.pallas_call(
        _mla_kv_kernel,
        grid=(s // bm,),
        in_specs=[pl.BlockSpec((bm, d), lambda i: (i, 0)), full(g), full(wc), full(kvg), full(wuk), full(wuv),
                  pl.BlockSpec((1, bm, LANES), lambda i: (1, i, 0)),
                  pl.BlockSpec((1, bm, LANES), lambda i: (1, i, 0))],
        out_specs=[pl.BlockSpec((bm, N_HEADS * QK_WIDTH), lambda i: (i, 0)),
                   pl.BlockSpec((N_HEADS * HEAD_DIM, bm), lambda i: (0, i))],
        out_shape=[jax.ShapeDtypeStruct((s, N_HEADS * QK_WIDTH), BF16),
                   jax.ShapeDtypeStruct((N_HEADS * HEAD_DIM, s), BF16)],
        compiler_params=pltpu.CompilerParams(
            dimension_semantics=("arbitrary",), vmem_limit_bytes=VMEM_LIMIT),
        name="mla_kv",
    )(x2, g, wc, kvg, wuk, wuv, cos, sin)


def _gate_kernel(qt_ref, k_ref, km_ref, qa_ref, ka_ref, *, nrow):
    i = pl.program_id(0)
    bm = k_ref.shape[0]
    nblk = km_ref.shape[0]
    blk_row = lax.broadcasted_iota(jnp.int32, (nrow, bm), 0)
    qblk = lax.shift_right_logical(i * bm + lax.broadcasted_iota(jnp.int32, (nrow, bm), 1), 8)
    rowf = blk_row.astype(F32)
    past = blk_row < qblk
    own = jnp.where(blk_row == qblk, 1.0, 0.0)
    zpad = jnp.zeros((nrow - nblk, LANES), F32)
    kcol = lax.broadcasted_iota(jnp.int32, (bm, LANES), 1)
    kblk = lax.shift_right_logical(i * bm + lax.broadcasted_iota(jnp.int32, (bm, LANES), 0), 8)
    onehot = jnp.where(kcol == kblk, 1.0, 0.0).astype(BF16)
    for h in range(N_HEADS):
        qt = qt_ref[h * HEAD_DIM:(h + 1) * HEAD_DIM, :]
        km = km_ref[:, h * LANES:(h + 1) * LANES]
        km = jnp.concatenate([km, zpad], axis=0).astype(BF16) if nrow > nblk else km.astype(BF16)
        gate = jnp.dot(km, qt, preferred_element_type=F32)
        g = jnp.where(past, gate, NEG)
        sel = own
        for t in range(MOBA_TOPK):
            m = jnp.max(g, axis=0, keepdims=True)
            idx = jnp.min(jnp.where(g == m, rowf, float(nrow)), axis=0, keepdims=True)
            pick = rowf == idx
            sel = jnp.where(pick, jnp.where(qblk > t, 1.0, sel), sel)
            g = jnp.where(pick, -jnp.inf, g)
        o0 = h * QK_WIDTH
        qa_ref[o0:o0 + HEAD_DIM, :] = qt
        qa_ref[o0 + HEAD_DIM:o0 + HEAD_DIM + nrow, :] = jnp.where(sel > 0.5, 0.0, NEG).astype(BF16)
        if HEAD_DIM + nrow < QK_WIDTH:
            qa_ref[o0 + HEAD_DIM + nrow:o0 + QK_WIDTH, :] = jnp.zeros((QK_WIDTH - HEAD_DIM - nrow, bm), BF16)
        ka_ref[:, (2 * h) * LANES:(2 * h + 1) * LANES] = k_ref[:, h * LANES:(h + 1) * LANES]
        ka_ref[:, (2 * h + 1) * LANES:(2 * h + 2) * LANES] = onehot


def _moba_gate(qt, k, kmean, bm=512):
    mw, s = qt.shape
    nblk = kmean.shape[0]
    nrow = -(-nblk // BF16_ROWS) * BF16_ROWS
    assert nrow <= LANES
    return pl.pallas_call(
        functools.partial(_gate_kernel, nrow=nrow),
        grid=(s // bm,),
        in_specs=[pl.BlockSpec((mw, bm), lambda i: (0, i)),
                  pl.BlockSpec((bm, mw), lambda i: (i, 0)),
                  pl.BlockSpec((nblk, mw), lambda i: (0, 0))],
        out_specs=[pl.BlockSpec((N_HEADS * QK_WIDTH, bm), lambda i: (0, i)),
                   pl.BlockSpec((bm, N_HEADS * QK_WIDTH), lambda i: (i, 0))],
        out_shape=[jax.ShapeDtypeStruct((N_HEADS * QK_WIDTH, s), BF16),
                   jax.ShapeDtypeStruct((s, N_HEADS * QK_WIDTH), BF16)],
        compiler_params=pltpu.CompilerParams(dimension_semantics=("arbitrary",)),
        name="moba_gate",
    )(qt, k, kmean)


def _flash_kernel(qt_ref, k_ref, vt_ref, o_ref, s_scr, cm_scr, m_scr, l_scr, acc_scr, *, ng, tg, tk):
    qi = pl.program_id(1)
    tq = ng * tg
    r = tk // tg
    m_scr[...] = jnp.full(m_scr.shape, M_INIT, F32)
    l_scr[...] = jnp.zeros(l_scr.shape, F32)
    acc_scr[...] = jnp.zeros(acc_scr.shape, F32)

    def start(buf, g, kj):
        k = k_ref[pl.ds(pl.multiple_of(kj * tk, tk), tk), :]
        st = jnp.dot(k, qt_ref[:, g * tg:(g + 1) * tg], preferred_element_type=F32)
        s_scr[buf] = st
        cm_scr[buf] = jnp.max(st, axis=0, keepdims=True)

    def finish(buf, g, kj, masked):
        st = s_scr[buf]
        if masked:
            key = kj * tk + lax.broadcasted_iota(jnp.int32, (tk, tg), 0)
            qry = qi * tq + g * tg + lax.broadcasted_iota(jnp.int32, (tk, tg), 1)
            st = jnp.where(key <= qry, st, NEG)
            cm = jnp.max(st, axis=0, keepdims=True)
        else:
            cm = cm_scr[buf]
        m_prev = m_scr[g]
        m_new = jnp.maximum(m_prev, cm)
        alpha = jnp.exp2(m_prev - m_new)
        p = jnp.exp2(st - m_new).astype(BF16)
        vt = vt_ref[:, pl.ds(pl.multiple_of(kj * tk, tk), tk)]
        vt1 = jnp.concatenate([vt, jnp.ones((ONES_ROWS, tk), BF16)], axis=0)
        pv = jnp.dot(vt1, p, preferred_element_type=F32)
        l_scr[g] = alpha * l_scr[g] + pv[HEAD_DIM:HEAD_DIM + 1, :]
        acc_scr[g] = alpha * acc_scr[g] + pv[:HEAD_DIM, :]
        m_scr[g] = m_new

    n_full = qi * (ng // r)
    tail = [(c, g, g < (c + 1) * r) for c in range(ng // r) for g in range(c * r, ng)]

    start(0, 0, 0)

    def chunk(kj):
        for g in range(ng):
            if g + 1 < ng:
                start((g + 1) % 2, g + 1, kj)
            else:
                start(0, 0, kj + 1)
            finish(g % 2, g, kj, False)

    unroll = 2 if (ng // r) % 2 == 0 else 1

    def body(t, c):
        for u in range(unroll):
            chunk(t * unroll + u)
        return c

    lax.fori_loop(0, n_full // unroll, body, 0)
    for i, (c, g, masked) in enumerate(tail):
        if i + 1 < len(tail):
            nc, ngp, _ = tail[i + 1]
            start((i + 1) % 2, ngp, n_full + nc)
        finish(i % 2, g, n_full + c, masked)
    for g in range(ng):
        o = acc_scr[g] / l_scr[g]
        o_ref[g * tg:(g + 1) * tg, :] = o.T.astype(o_ref.dtype)


def _flash(qt, k, vt, ng=4, tg=512, tk=512):
    s = k.shape[0]
    tq = ng * tg
    assert ng % 2 == 0 and tk % tg == 0 and tq % tk == 0
    kern = functools.partial(_flash_kernel, ng=ng, tg=tg, tk=tk)
    return pl.pallas_call(
        kern,
        grid=(N_HEADS, s // tq),
        in_specs=[pl.BlockSpec((QK_WIDTH, tq), lambda h, i: (h, i)),
                  pl.BlockSpec((s, QK_WIDTH), lambda h, i: (0, h)),
                  pl.BlockSpec((HEAD_DIM, s), lambda h, i: (h, 0))],
        out_specs=pl.BlockSpec((tq, HEAD_DIM), lambda h, i: (i, h)),
        out_shape=jax.ShapeDtypeStruct((s, N_HEADS * HEAD_DIM), BF16),
        scratch_shapes=[pltpu.VMEM((2, tk, tg), F32), pltpu.VMEM((2, 1, tg), F32),
                        pltpu.VMEM((ng, 1, tg), F32), pltpu.VMEM((ng, 1, tg), F32),
                        pltpu.VMEM((ng, HEAD_DIM, tg), F32)],
        compiler_params=pltpu.CompilerParams(
            dimension_semantics=("arbitrary", "arbitrary"), vmem_limit_bytes=VMEM_LIMIT),
        name="flash_attn",
    )(qt, k, vt)


def _wo_kernel(x_ref, a_ref, b_ref, wa_ref, wb_ref, o_ref):
    o_ref[...] = (x_ref[...]
                  + jnp.dot(a_ref[...], wa_ref[...], preferred_element_type=F32)
                  + jnp.dot(b_ref[...], wb_ref[...], preferred_element_type=F32))


def _out_proj(x2, oa, ob, wa, wb, bm=1024, bn=1024):
    s, d = x2.shape
    ka, kb = oa.shape[1], ob.shape[1]
    return pl.pallas_call(
        _wo_kernel,
        grid=(s // bm, d // bn),
        in_specs=[pl.BlockSpec((bm, bn), lambda i, j: (i, j)),
                  pl.BlockSpec((bm, ka), lambda i, j: (i, 0)),
                  pl.BlockSpec((bm, kb), lambda i, j: (i, 0)),
                  pl.BlockSpec((ka, bn), lambda i, j: (0, j)),
                  pl.BlockSpec((kb, bn), lambda i, j: (0, j))],
        out_specs=pl.BlockSpec((bm, bn), lambda i, j: (i, j)),
        out_shape=jax.ShapeDtypeStruct((s, d), F32),
        compiler_params=pltpu.CompilerParams(
            dimension_semantics=("arbitrary", "arbitrary"), vmem_limit_bytes=VMEM_LIMIT),
        name="out_proj",
    )(x2, oa, ob, wa, wb)


def _ffn_kernel(h_ref, g_ref, wg_ref, wu_ref, wd_ref, o_ref, f_scr):
    k = pl.program_id(1)

    @pl.when(k == 0)
    def _():
        h = h_ref[...]
        f_scr[...] = _rms(h, g_ref[...]).astype(BF16)
        o_ref[...] = h

    f = f_scr[...]
    gt = jnp.dot(f, wg_ref[...], preferred_element_type=F32)
    up = jnp.dot(f, wu_ref[...], preferred_element_type=F32)
    act = (gt * jax.nn.sigmoid(gt) * up).astype(BF16)
    o_ref[...] += jnp.dot(act, wd_ref[...], preferred_element_type=F32)


def _ffn(h, g, wg, wu, wd, bm=512, bf=512):
    s, d = h.shape
    f = wg.shape[1]
    return pl.pallas_call(
        _ffn_kernel,
        grid=(s // bm, f // bf),
        in_specs=[pl.BlockSpec((bm, d), lambda i, k: (i, 0)),
                  pl.BlockSpec((1, d), lambda i, k: (0, 0)),
                  pl.BlockSpec((d, bf), lambda i, k: (0, k)),
                  pl.BlockSpec((d, bf), lambda i, k: (0, k)),
                  pl.BlockSpec((bf, d), lambda i, k: (k, 0))],
        out_specs=pl.BlockSpec((bm, d), lambda i, k: (i, 0)),
        out_shape=jax.ShapeDtypeStruct((s, d), F32),
        scratch_shapes=[pltpu.VMEM((bm, d), BF16)],
        compiler_params=pltpu.CompilerParams(
            dimension_semantics=("arbitrary", "arbitrary"), vmem_limit_bytes=VMEM_LIMIT),
        name="swiglu",
    )(h, g, wg, wu, wd)


def _ple_kernel(h_ref, p_ref, gp_ref, wpg_ref, wpe_ref, gf_ref, o_ref):
    h = h_ref[...]
    r = _rms(h, gp_ref[...]).astype(BF16)
    gate = jax.nn.sigmoid(jnp.dot(r, wpg_ref[...], preferred_element_type=F32))
    pe = jnp.dot(p_ref[...].astype(BF16), wpe_ref[...], preferred_element_type=F32)
    o_ref[...] = _rms(h + gate * pe, gf_ref[...])


def _ple_final(h, p2, gp, wpg, wpe, gf, bm=512):
    s, d = h.shape
    pd = p2.shape[1]
    full = lambda a: pl.BlockSpec(a.shape, lambda i: (0,) * a.ndim)
    return pl.pallas_call(
        _ple_kernel,
        grid=(s // bm,),
        in_specs=[pl.BlockSpec((bm, d), lambda i: (i, 0)),
                  pl.BlockSpec((bm, pd), lambda i: (i, 0)),
                  full(gp), full(wpg), full(wpe), full(gf)],
        out_specs=pl.BlockSpec((bm, d), lambda i: (i, 0)),
        out_shape=jax.ShapeDtypeStruct((s, d), F32),
        compiler_params=pltpu.CompilerParams(
            dimension_semantics=("arbitrary",), vmem_limit_bytes=VMEM_LIMIT),
        name="ple_final",
    )(h, p2, gp, wpg, wpe, gf)


def kernel(x, p, positions, attn_norm, w_in, kv_norm, w_ukv, w_o, ffn_norm, w_gate, w_up, w_down,
           ple_norm, w_ple_gate, w_ple_proj, final_norm):
    b, s, d = x.shape
    assert b == 1 and p.shape[0] == 1 and s % 2048 == 0
    x2 = x.reshape(s, d)
    p2 = p.reshape(s, p.shape[-1])
    row = lambda v: v.reshape(1, -1).astype(F32)

    q_cols = N_HEADS * (HEAD_DIM + MLA_ROPE)
    c1 = q_cols + KV_RANK
    c2 = c1 + MLA_ROPE
    w_in0 = w_in[0].astype(BF16)
    w_q = w_in0[:, :q_cols]
    w_ckv = jnp.pad(w_in0[:, q_cols:c2], ((0, 0), (0, LANES - MLA_ROPE)))
    w_mb = w_in0[:, c2:]
    wukv = w_ukv[0].reshape(KV_RANK, N_HEADS, 2 * HEAD_DIM)
    w_uk = wukv[:, :, :HEAD_DIM].reshape(KV_RANK, -1).astype(BF16)
    w_uv = wukv[:, :, HEAD_DIM:].reshape(KV_RANK, -1).astype(BF16)
    mla_out = N_HEADS * HEAD_DIM
    wo_a = w_o[0][:mla_out].astype(BF16)
    wo_b = w_o[0][mla_out:].astype(BF16)

    ct, st, cos, sin = _rope_tables(positions)
    qt_mla, qt_mb, k_mb, kmean, vt_mb = _in_proj(x2, row(attn_norm[0]), w_q, w_mb, ct, st, cos, sin)
    k_mla, vt_mla = _mla_kv(x2, row(attn_norm[0]), w_ckv, row(kv_norm[0]), w_uk, w_uv, cos, sin)

    kmean = kmean.reshape(s // MOBA_BLOCK, N_HEADS * HEAD_DIM)
    qat_mb, k_aug = _moba_gate(qt_mb, k_mb, kmean)

    out_mla = _flash(qt_mla, k_mla, vt_mla)
    out_mb = _flash(qat_mb, k_aug, vt_mb)

    h1 = _out_proj(x2, out_mla, out_mb, wo_a, wo_b)
    h2 = _ffn(h1, row(ffn_norm[0]), w_gate[0].astype(BF16), w_up[0].astype(BF16), w_down[0].astype(BF16))
    out = _ple_final(h2, p2, row(ple_norm[0]), w_ple_gate[0].astype(BF16), w_ple_proj[0].astype(BF16),
                     row(final_norm))
    return out.reshape(b, s, d)
```

```python
import functools

import jax
import jax.numpy as jnp
from jax import lax
from jax.experimental import pallas as pl
from jax.experimental.pallas import tpu as pltpu

F32 = jnp.float32
BF16 = jnp.bfloat16

EPS = 1e-6
NEG = -1e30
M_INIT = -3.0e38
ROPE_THETA = 10000.0
LOG2E = 1.4426950408889634

LANES = 128
BF16_ROWS = 16
HEAD_DIM = 128
MLA_ROPE = 64
KV_RANK = 512
N_HEADS = 8
MOBA_BLOCK = 256
MOBA_TOPK = 3
QK_WIDTH = 2 * LANES
ONES_ROWS = BF16_ROWS
VMEM_LIMIT = 56 * 1024 * 1024

_NT = (((1,), (1,)), ((), ()))
_TN_T = (((0,), (1,)), ((), ()))


def _rms(x, g):
    ms = jnp.mean(x * x, axis=-1, keepdims=True)
    return x * lax.rsqrt(ms + EPS) * g


def _rope(slab, cos, sin):
    return slab * cos + pltpu.roll(slab, LANES // 2, axis=1) * sin


def _rope_t(x1, x2, c, s):
    return x1 * c - x2 * s, x2 * c + x1 * s


def _rope_table_kernel(pos_ref, inv_ref, ct_ref, st_ref, cos_ref, sin_ref):
    ang = inv_ref[...] * pos_ref[...].astype(F32)
    ct = jnp.cos(ang)
    st = jnp.sin(ang)
    ct_ref[...] = ct
    st_ref[...] = st
    c = ct.T
    s = st.T
    half, quart = LANES // 2, LANES // 4
    lane = lax.broadcasted_iota(jnp.int32, c.shape, 1)
    low = lane < half
    c_sw = pltpu.roll(c, half, axis=1)
    s_sw = pltpu.roll(s, half, axis=1)
    cos_ref[0] = jnp.where(low, c, c_sw)
    sin_ref[0] = jnp.where(low, -s, s_sw)
    c_q = pltpu.roll(c_sw, quart, axis=1)
    s_q = pltpu.roll(s_sw, quart, axis=1)
    first = lane < quart
    cos_ref[1] = jnp.where(first, c_sw, c_q)
    sin_ref[1] = jnp.where(first, -s_sw, s_q)


def _rope_tables(positions, bm=1024):
    s = positions.shape[-1]
    pos = positions.reshape(1, s)
    half_mb = HEAD_DIM // 2
    half_ml = MLA_ROPE // 2
    inv_mb = ROPE_THETA ** (-(jnp.arange(half_mb, dtype=F32) * 2.0 / HEAD_DIM))
    inv_ml = ROPE_THETA ** (-(jnp.arange(half_ml, dtype=F32) * 2.0 / MLA_ROPE))
    inv = jnp.concatenate([inv_mb, inv_ml, jnp.zeros((LANES - half_mb - half_ml,), F32)]).reshape(LANES, 1)
    t_out = jax.ShapeDtypeStruct((LANES, s), F32)
    r_out = jax.ShapeDtypeStruct((2, s, LANES), F32)
    return pl.pallas_call(
        _rope_table_kernel,
        grid=(s // bm,),
        in_specs=[pl.BlockSpec((1, bm), lambda i: (0, i)),
                  pl.BlockSpec((LANES, 1), lambda i: (0, 0))],
        out_specs=[pl.BlockSpec((LANES, bm), lambda i: (0, i))] * 2
                  + [pl.BlockSpec((2, bm, LANES), lambda i: (0, i, 0))] * 2,
        out_shape=[t_out, t_out, r_out, r_out],
        name="rope_tables",
    )(pos, inv)


def _norm_kernel(x_ref, g_ref, a_ref):
    a_ref[...] = _rms(x_ref[...], g_ref[...]).astype(a_ref.dtype)


def _attn_norm(x2, g, bm=1024):
    s, d = x2.shape
    return pl.pallas_call(
        _norm_kernel,
        grid=(s // bm,),
        in_specs=[pl.BlockSpec((bm, d), lambda i: (i, 0)), pl.BlockSpec((1, d), lambda i: (0, 0))],
        out_specs=pl.BlockSpec((bm, d), lambda i: (i, 0)),
        out_shape=jax.ShapeDtypeStruct((s, d), BF16),
        compiler_params=pltpu.CompilerParams(dimension_semantics=("arbitrary",), vmem_limit_bytes=VMEM_LIMIT),
        name="attn_norm",
    )(x2, g)


def _proj_kernel(a_ref, wq_ref, wm_ref, ct_ref, st_ref, cos_ref, sin_ref,
                 qtl_ref, qtm_ref, k_ref, km_ref, vt_ref, wt_scr, *, scale_mla, scale_mb):
    j = pl.program_id(0)
    i = pl.program_id(1)
    bm = a_ref.shape[0]
    half_mb, half_ml = HEAD_DIM // 2, MLA_ROPE // 2
    hq = HEAD_DIM + MLA_ROPE
    nq = wq_ref.shape[1]

    @pl.when((i == 0) & (j < 2))
    def _():
        wt_scr[0:nq, :] = wq_ref[...].T

    @pl.when((i == 0) & ((j == 2) | (j == 4)))
    def _():
        wt_scr[...] = wm_ref[...].T

    @pl.when(j < 2)
    def _():
        t = lax.dot_general(wt_scr[0:nq, :], a_ref[...], _NT, preferred_element_type=F32)
        c = ct_ref[half_mb:half_mb + half_ml, :]
        s = st_ref[half_mb:half_mb + half_ml, :]
        for h in range(nq // hq):
            r0, o0 = h * hq, h * QK_WIDTH
            qtl_ref[o0:o0 + HEAD_DIM, :] = (t[r0:r0 + HEAD_DIM, :] * scale_mla).astype(BF16)
            x1 = t[r0 + HEAD_DIM:r0 + HEAD_DIM + half_ml, :]
            x2 = t[r0 + HEAD_DIM + half_ml:r0 + hq, :]
            o1, o2 = _rope_t(x1, x2, c, s)
            qtl_ref[o0 + HEAD_DIM:o0 + HEAD_DIM + half_ml, :] = (o1 * scale_mla).astype(BF16)
            qtl_ref[o0 + HEAD_DIM + half_ml:o0 + hq, :] = (o2 * scale_mla).astype(BF16)
            qtl_ref[o0 + hq:o0 + QK_WIDTH, :] = jnp.zeros((QK_WIDTH - hq, bm), BF16)

    @pl.when(j == 2)
    def _():
        t = lax.dot_general(wt_scr[...], a_ref[...], _NT, preferred_element_type=F32)
        c = ct_ref[0:half_mb, :]
        s = st_ref[0:half_mb, :]
        for h in range(N_HEADS):
            r0 = h * HEAD_DIM
            o1, o2 = _rope_t(t[r0:r0 + half_mb, :], t[r0 + half_mb:r0 + HEAD_DIM, :], c, s)
            qtm_ref[r0:r0 + half_mb, :] = (o1 * scale_mb).astype(BF16)
            qtm_ref[r0 + half_mb:r0 + HEAD_DIM, :] = (o2 * scale_mb).astype(BF16)

    @pl.when(j == 3)
    def _():
        acc = jnp.dot(a_ref[...], wm_ref[...], preferred_element_type=F32)
        nb = bm // MOBA_BLOCK
        for h in range(N_HEADS):
            r = _rope(acc[:, h * LANES:(h + 1) * LANES], cos_ref[0], sin_ref[0])
            k_ref[:, h * LANES:(h + 1) * LANES] = r.astype(BF16)
            km = r.reshape(nb, MOBA_BLOCK, LANES).sum(axis=1) * (1.0 / MOBA_BLOCK)
            km_ref[0, :, h * LANES:(h + 1) * LANES] = km

    @pl.when(j == 4)
    def _():
        vt = lax.dot_general(wt_scr[...], a_ref[...], _NT, preferred_element_type=F32)
        vt_ref[...] = vt.astype(BF16)


def _in_proj(a, wq, wm, ct, st, cos, sin, bm=1024):
    s, d = a.shape
    mw = N_HEADS * HEAD_DIM
    hq = HEAD_DIM + MLA_ROPE
    nq = wq.shape[1] // 2
    assert wq.shape[1] == N_HEADS * hq and wm.shape[1] == 3 * mw and nq <= mw
    nb = bm // MOBA_BLOCK
    ni = s // bm
    kern = functools.partial(_proj_kernel, scale_mla=float(hq ** -0.5 * LOG2E),
                             scale_mb=float(HEAD_DIM ** -0.5 * LOG2E))
    qw = N_HEADS * QK_WIDTH

    def rows(first, last):
        return lambda j, i: jnp.where(j < first, 0, jnp.where(j > last, ni - 1, i))

    q_i, m_i, k_i, v_i = rows(0, 1), rows(2, 2), rows(3, 3), rows(4, 4)
    return pl.pallas_call(
        kern,
        grid=(5, ni),
        in_specs=[pl.BlockSpec((bm, d), lambda j, i: (i, 0)),
                  pl.BlockSpec((d, nq), lambda j, i: (0, jnp.minimum(j, 1))),
                  pl.BlockSpec((d, mw), lambda j, i: (0, jnp.clip(j - 2, 0, 2))),
                  pl.BlockSpec((LANES, bm), lambda j, i: (0, i)),
                  pl.BlockSpec((LANES, bm), lambda j, i: (0, i)),
                  pl.BlockSpec((2, bm, LANES), lambda j, i: (0, i, 0)),
                  pl.BlockSpec((2, bm, LANES), lambda j, i: (0, i, 0))],
        out_specs=[pl.BlockSpec((qw // 2, bm), lambda j, i: (jnp.minimum(j, 1), q_i(j, i))),
                   pl.BlockSpec((mw, bm), lambda j, i: (0, m_i(j, i))),
                   pl.BlockSpec((bm, mw), lambda j, i: (k_i(j, i), 0)),
                   pl.BlockSpec((1, nb, mw), lambda j, i: (k_i(j, i), 0, 0)),
                   pl.BlockSpec((mw, bm), lambda j, i: (0, v_i(j, i)))],
        out_shape=[jax.ShapeDtypeStruct((qw, s), BF16),
                   jax.ShapeDtypeStruct((mw, s), BF16),
                   jax.ShapeDtypeStruct((s, mw), BF16),
                   jax.ShapeDtypeStruct((ni, nb, mw), F32),
                   jax.ShapeDtypeStruct((mw, s), BF16)],
        scratch_shapes=[pltpu.VMEM((mw, d), BF16)],
        compiler_params=pltpu.CompilerParams(
            dimension_semantics=("arbitrary", "arbitrary"), vmem_limit_bytes=VMEM_LIMIT),
        name="in_proj",
    )(a, wq, wm, ct, st, cos, sin)


def _mla_kv_kernel(a_ref, wc_ref, kvg_ref, wuk_ref, wuv_ref, cos_ref, sin_ref, k_ref, vt_ref):
    t = jnp.dot(a_ref[...], wc_ref[...], preferred_element_type=F32)
    cn = _rms(t[:, :KV_RANK], kvg_ref[...]).astype(BF16)
    kn = jnp.dot(cn, wuk_ref[...], preferred_element_type=F32)
    vt = lax.dot_general(wuv_ref[...], cn, _TN_T, preferred_element_type=F32)
    slab = t[:, KV_RANK:]
    quart = LANES // 4
    lane = lax.broadcasted_iota(jnp.int32, slab.shape, 1)
    partner = jnp.where(lane < quart, pltpu.roll(slab, LANES - quart, axis=1), pltpu.roll(slab, quart, axis=1))
    kpe = (slab * cos_ref[0] + partner * sin_ref[0]).astype(BF16)
    for h in range(N_HEADS):
        k_ref[:, (2 * h) * LANES:(2 * h + 1) * LANES] = kn[:, h * LANES:(h + 1) * LANES].astype(BF16)
        k_ref[:, (2 * h + 1) * LANES:(2 * h + 2) * LANES] = kpe
    vt_ref[...] = vt.astype(BF16)


def _mla_kv(a, wc, kvg, wuk, wuv, cos, sin, bm=512):
    s, d = a.shape
    full = lambda w: pl.BlockSpec(w.shape, lambda i: (0,) * w.ndim)
    return pl.pallas_call(
        _mla_kv_kernel,
        grid=(s // bm,),
        in_specs=[pl.BlockSpec((bm, d), lambda i: (i, 0)), full(wc), full(kvg), full(wuk), full(wuv),
                  pl.BlockSpec((1, bm, LANES), lambda i: (1, i, 0)),
                  pl.BlockSpec((1, bm, LANES), lambda i: (1, i, 0))],
        out_specs=[pl.BlockSpec((bm, N_HEADS * QK_WIDTH), lambda i: (i, 0)),
                   pl.BlockSpec((N_HEADS * HEAD_DIM, bm), lambda i: (0, i))],
        out_shape=[jax.ShapeDtypeStruct((s, N_HEADS * QK_WIDTH), BF16),
                   jax.ShapeDtypeStruct((N_HEADS * HEAD_DIM, s), BF16)],
        compiler_params=pltpu.CompilerParams(
            dimension_semantics=("arbitrary",), vmem_limit_bytes=VMEM_LIMIT),
        name="mla_kv",
    )(a, wc, kvg, wuk, wuv, cos, sin)


def _gate_kernel(qt_ref, k_ref, km_ref, qa_ref, ka_ref, *, nrow):
    i = pl.program_id(0)
    bm = k_ref.shape[0]
    nblk = km_ref.shape[0]
    blk_row = lax.broadcasted_iota(jnp.int32, (nrow, bm), 0)
    qblk = lax.shift_right_logical(i * bm + lax.broadcasted_iota(jnp.int32, (nrow, bm), 1), 8)
    rowf = blk_row.astype(F32)
    past = blk_row < qblk
    own = jnp.where(blk_row == qblk, 1.0, 0.0)
    zpad = jnp.zeros((nrow - nblk, LANES), F32)
    kcol = lax.broadcasted_iota(jnp.int32, (bm, LANES), 1)
    kblk = lax.shift_right_logical(i * bm + lax.broadcasted_iota(jnp.int32, (bm, LANES), 0), 8)
    onehot = jnp.where(kcol == kblk, 1.0, 0.0).astype(BF16)
    for h in range(N_HEADS):
        qt = qt_ref[h * HEAD_DIM:(h + 1) * HEAD_DIM, :]
        km = km_ref[:, h * LANES:(h + 1) * LANES]
        km = jnp.concatenate([km, zpad], axis=0).astype(BF16) if nrow > nblk else km.astype(BF16)
        gate = jnp.dot(km, qt, preferred_element_type=F32)
        g = jnp.where(past, gate, NEG)
        sel = own
        for t in range(MOBA_TOPK):
            m = jnp.max(g, axis=0, keepdims=True)
            idx = jnp.min(jnp.where(g == m, rowf, float(nrow)), axis=0, keepdims=True)
            pick = rowf == idx
            sel = jnp.where(pick, jnp.where(qblk > t, 1.0, sel), sel)
            g = jnp.where(pick, -jnp.inf, g)
        o0 = h * QK_WIDTH
        qa_ref[o0:o0 + HEAD_DIM, :] = qt
        qa_ref[o0 + HEAD_DIM:o0 + HEAD_DIM + nrow, :] = jnp.where(sel > 0.5, 0.0, NEG).astype(BF16)
        if HEAD_DIM + nrow < QK_WIDTH:
            qa_ref[o0 + HEAD_DIM + nrow:o0 + QK_WIDTH, :] = jnp.zeros((QK_WIDTH - HEAD_DIM - nrow, bm), BF16)
        ka_ref[:, (2 * h) * LANES:(2 * h + 1) * LANES] = k_ref[:, h * LANES:(h + 1) * LANES]
        ka_ref[:, (2 * h + 1) * LANES:(2 * h + 2) * LANES] = onehot


def _moba_gate(qt, k, kmean, bm=512):
    mw, s = qt.shape
    nblk = kmean.shape[0]
    nrow = -(-nblk // BF16_ROWS) * BF16_ROWS
    assert nrow <= LANES
    return pl.pallas_call(
        functools.partial(_gate_kernel, nrow=nrow),
        grid=(s // bm,),
        in_specs=[pl.BlockSpec((mw, bm), lambda i: (0, i)),
                  pl.BlockSpec((bm, mw), lambda i: (i, 0)),
                  pl.BlockSpec((nblk, mw), lambda i: (0, 0))],
        out_specs=[pl.BlockSpec((N_HEADS * QK_WIDTH, bm), lambda i: (0, i)),
                   pl.BlockSpec((bm, N_HEADS * QK_WIDTH), lambda i: (i, 0))],
        out_shape=[jax.ShapeDtypeStruct((N_HEADS * QK_WIDTH, s), BF16),
                   jax.ShapeDtypeStruct((s, N_HEADS * QK_WIDTH), BF16)],
        compiler_params=pltpu.CompilerParams(dimension_semantics=("arbitrary",)),
        name="moba_gate",
    )(qt, k, kmean)


def _flash_kernel(qt_ref, k_ref, vt_ref, o_ref, s_scr, cm_scr, m_scr, l_scr, acc_scr, *, ng, tg, tk):
    qi = pl.program_id(1)
    tq = ng * tg
    r = tk // tg
    m_scr[...] = jnp.full(m_scr.shape, M_INIT, F32)
    l_scr[...] = jnp.zeros(l_scr.shape, F32)
    acc_scr[...] = jnp.zeros(acc_scr.shape, F32)

    def start(buf, g, kj):
        k = k_ref[pl.ds(pl.multiple_of(kj * tk, tk), tk), :]
        st = jnp.dot(k, qt_ref[:, g * tg:(g + 1) * tg], preferred_element_type=F32)
        s_scr[buf] = st
        cm_scr[buf] = jnp.max(st, axis=0, keepdims=True)

    def finish(buf, g, kj, masked):
        st = s_scr[buf]
        if masked:
            key = kj * tk + lax.broadcasted_iota(jnp.int32, (tk, tg), 0)
            qry = qi * tq + g * tg + lax.broadcasted_iota(jnp.int32, (tk, tg), 1)
            st = jnp.where(key <= qry, st, NEG)
            cm = jnp.max(st, axis=0, keepdims=True)
        else:
            cm = cm_scr[buf]
        m_prev = m_scr[g]
        m_new = jnp.maximum(m_prev, cm)
        alpha = jnp.exp2(m_prev - m_new)
        p = jnp.exp2(st - m_new).astype(BF16)
        vt = vt_ref[:, pl.ds(pl.multiple_of(kj * tk, tk), tk)]
        vt1 = jnp.concatenate([vt, jnp.ones((ONES_ROWS, tk), BF16)], axis=0)
        pv = jnp.dot(vt1, p, preferred_element_type=F32)
        l_scr[g] = alpha * l_scr[g] + pv[HEAD_DIM:HEAD_DIM + 1, :]
        acc_scr[g] = alpha * acc_scr[g] + pv[:HEAD_DIM, :]
        m_scr[g] = m_new

    n_full = qi * (ng // r)
    tail = [(c, g, g < (c + 1) * r) for c in range(ng // r) for g in range(c * r, ng)]

    start(0, 0, 0)

    def chunk(kj):
        for g in range(ng):
            if g + 1 < ng:
                start((g + 1) % 2, g + 1, kj)
            else:
                start(0, 0, kj + 1)
            finish(g % 2, g, kj, False)

    unroll = 2 if (ng // r) % 2 == 0 else 1

    def body(t, c):
        for u in range(unroll):
            chunk(t * unroll + u)
        return c

    lax.fori_loop(0, n_full // unroll, body, 0)
    for i, (c, g, masked) in enumerate(tail):
        if i + 1 < len(tail):
            nc, ngp, _ = tail[i + 1]
            start((i + 1) % 2, ngp, n_full + nc)
        finish(i % 2, g, n_full + c, masked)
    for g in range(ng):
        o = acc_scr[g] / l_scr[g]
        o_ref[g * tg:(g + 1) * tg, :] = o.T.astype(o_ref.dtype)


def _flash(qt, k, vt, ng=4, tg=512, tk=512):
    s = k.shape[0]
    tq = ng * tg
    assert ng % 2 == 0 and tk % tg == 0 and tq % tk == 0
    kern = functools.partial(_flash_kernel, ng=ng, tg=tg, tk=tk)
    return pl.pallas_call(
        kern,
        grid=(N_HEADS, s // tq),
        in_specs=[pl.BlockSpec((QK_WIDTH, tq), lambda h, i: (h, i)),
                  pl.BlockSpec((s, QK_WIDTH), lambda h, i: (0, h)),
                  pl.BlockSpec((HEAD_DIM, s), lambda h, i: (h, 0))],
        out_specs=pl.BlockSpec((tq, HEAD_DIM), lambda h, i: (i, h)),
        out_shape=jax.ShapeDtypeStruct((s, N_HEADS * HEAD_DIM), BF16),
        scratch_shapes=[pltpu.VMEM((2, tk, tg), F32), pltpu.VMEM((2, 1, tg), F32),
                        pltpu.VMEM((ng, 1, tg), F32), pltpu.VMEM((ng, 1, tg), F32),
                        pltpu.VMEM((ng, HEAD_DIM, tg), F32)],
        compiler_params=pltpu.CompilerParams(
            dimension_semantics=("arbitrary", "arbitrary"), vmem_limit_bytes=VMEM_LIMIT),
        name="flash_attn",
    )(qt, k, vt)


def _wo_kernel(x_ref, a_ref, b_ref, wa_ref, wb_ref, o_ref):
    o_ref[...] = (x_ref[...]
                  + jnp.dot(a_ref[...], wa_ref[...], preferred_element_type=F32)
                  + jnp.dot(b_ref[...], wb_ref[...], preferred_element_type=F32))


def _out_proj(x2, oa, ob, wa, wb, bm=1024, bn=1024):
    s, d = x2.shape
    ka, kb = oa.shape[1], ob.shape[1]
    return pl.pallas_call(
        _wo_kernel,
        grid=(s // bm, d // bn),
        in_specs=[pl.BlockSpec((bm, bn), lambda i, j: (i, j)),
                  pl.BlockSpec((bm, ka), lambda i, j: (i, 0)),
                  pl.BlockSpec((bm, kb), lambda i, j: (i, 0)),
                  pl.BlockSpec((ka, bn), lambda i, j: (0, j)),
                  pl.BlockSpec((kb, bn), lambda i, j: (0, j))],
        out_specs=pl.BlockSpec((bm, bn), lambda i, j: (i, j)),
        out_shape=jax.ShapeDtypeStruct((s, d), F32),
        compiler_params=pltpu.CompilerParams(
            dimension_semantics=("arbitrary", "arbitrary"), vmem_limit_bytes=VMEM_LIMIT),
        name="out_proj",
    )(x2, oa, ob, wa, wb)


def _ffn_kernel(h_ref, g_ref, wg_ref, wu_ref, wd_ref, o_ref, f_scr):
    k = pl.program_id(1)

    @pl.when(k == 0)
    def _():
        h = h_ref[...]
        f_scr[...] = _rms(h, g_ref[...]).astype(BF16)
        o_ref[...] = h

    f = f_scr[...]
    gt = jnp.dot(f, wg_ref[...], preferred_element_type=F32)
    up = jnp.dot(f, wu_ref[...], preferred_element_type=F32)
    act = (gt * jax.nn.sigmoid(gt) * up).astype(BF16)
    o_ref[...] += jnp.dot(act, wd_ref[...], preferred_element_type=F32)


def _ffn(h, g, wg, wu, wd, bm=512, bf=512):
    s, d = h.shape
    f = wg.shape[1]
    return pl.pallas_call(
        _ffn_kernel,
        grid=(s // bm, f // bf),
        in_specs=[pl.BlockSpec((bm, d), lambda i, k: (i, 0)),
                  pl.BlockSpec((1, d), lambda i, k: (0, 0)),
                  pl.BlockSpec((d, bf), lambda i, k: (0, k)),
                  pl.BlockSpec((d, bf), lambda i, k: (0, k)),
                  pl.BlockSpec((bf, d), lambda i, k: (k, 0))],
        out_specs=pl.BlockSpec((bm, d), lambda i, k: (i, 0)),
        out_shape=jax.ShapeDtypeStruct((s, d), F32),
        scratch_shapes=[pltpu.VMEM((bm, d), BF16)],
        compiler_params=pltpu.CompilerParams(
            dimension_semantics=("arbitrary", "arbitrary"), vmem_limit_bytes=VMEM_LIMIT),
        name="swiglu",
    )(h, g, wg, wu, wd)


def _ple_kernel(h_ref, p_ref, gp_ref, wpg_ref, wpe_ref, gf_ref, o_ref):
    h = h_ref[...]
    r = _rms(h, gp_ref[...]).astype(BF16)
    gate = jax.nn.sigmoid(jnp.dot(r, wpg_ref[...], preferred_element_type=F32))
    pe = jnp.dot(p_ref[...].astype(BF16), wpe_ref[...], preferred_element_type=F32)
    o_ref[...] = _rms(h + gate * pe, gf_ref[...])


def _ple_final(h, p2, gp, wpg, wpe, gf, bm=512):
    s, d = h.shape
    pd = p2.shape[1]
    full = lambda a: pl.BlockSpec(a.shape, lambda i: (0,) * a.ndim)
    return pl.pallas_call(
        _ple_kernel,
        grid=(s // bm,),
        in_specs=[pl.BlockSpec((bm, d), lambda i: (i, 0)),
                  pl.BlockSpec((bm, pd), lambda i: (i, 0)),
                  full(gp), full(wpg), full(wpe), full(gf)],
        out_specs=pl.BlockSpec((bm, d), lambda i: (i, 0)),
        out_shape=jax.ShapeDtypeStruct((s, d), F32),
        compiler_params=pltpu.CompilerParams(
            dimension_semantics=("arbitrary",), vmem_limit_bytes=VMEM_LIMIT),
        name="ple_final",
    )(h, p2, gp, wpg, wpe, gf)


def kernel(x, p, positions, attn_norm, w_in, kv_norm, w_ukv, w_o, ffn_norm, w_gate, w_up, w_down,
           ple_norm, w_ple_gate, w_ple_proj, final_norm):
    b, s, d = x.shape
    assert b == 1 and p.shape[0] == 1 and s % 2048 == 0
    x2 = x.reshape(s, d)
    p2 = p.reshape(s, p.shape[-1])
    row = lambda v: v.reshape(1, -1).astype(F32)

    q_cols = N_HEADS * (HEAD_DIM + MLA_ROPE)
    c1 = q_cols + KV_RANK
    c2 = c1 + MLA_ROPE
    w_in0 = w_in[0].astype(BF16)
    w_q = w_in0[:, :q_cols]
    w_ckv = jnp.pad(w_in0[:, q_cols:c2], ((0, 0), (0, LANES - MLA_ROPE)))
    w_mb = w_in0[:, c2:]
    wukv = w_ukv[0].reshape(KV_RANK, N_HEADS, 2 * HEAD_DIM)
    w_uk = wukv[:, :, :HEAD_DIM].reshape(KV_RANK, -1).astype(BF16)
    w_uv = wukv[:, :, HEAD_DIM:].reshape(KV_RANK, -1).astype(BF16)
    mla_out = N_HEADS * HEAD_DIM
    wo_a = w_o[0][:mla_out].astype(BF16)
    wo_b = w_o[0][mla_out:].astype(BF16)

    ct, st, cos, sin = _rope_tables(positions)
    a = _attn_norm(x2, row(attn_norm[0]))
    qt_mla, qt_mb, k_mb, kmean, vt_mb = _in_proj(a, w_q, w_mb, ct, st, cos, sin)
    k_mla, vt_mla = _mla_kv(a, w_ckv, row(kv_norm[0]), w_uk, w_uv, cos, sin)

    kmean = kmean.reshape(s // MOBA_BLOCK, N_HEADS * HEAD_DIM)
    qat_mb, k_aug = _moba_gate(qt_mb, k_mb, kmean)

    out_mla = _flash(qt_mla, k_mla, vt_mla)
    out_mb = _flash(qat_mb, k_aug, vt_mb)

    h1 = _out_proj(x2, out_mla, out_mb, wo_a, wo_b)
    h2 = _ffn(h1, row(ffn_norm[0]), w_gate[0].astype(BF16), w_up[0].astype(BF16), w_down[0].astype(BF16))
    out = _ple_final(h2, p2, row(ple_norm[0]), w_ple_gate[0].astype(BF16), w_ple_proj[0].astype(BF16),
                     row(final_norm))
    return out.reshape(b, s, d)
```

```python
import functools

import jax
import jax.numpy as jnp
from jax import lax
from jax.experimental import pallas as pl
from jax.experimental.pallas import tpu as pltpu

F32 = jnp.float32
BF16 = jnp.bfloat16

EPS = 1e-6
NEG = -1e30
M_INIT = -3.0e38
ROPE_THETA = 10000.0
LOG2E = 1.4426950408889634

LANES = 128
BF16_ROWS = 16
HEAD_DIM = 128
MLA_ROPE = 64
KV_RANK = 512
N_HEADS = 8
MOBA_BLOCK = 256
MOBA_TOPK = 3
QK_WIDTH = 2 * LANES
ONES_ROWS = BF16_ROWS
VMEM_LIMIT = 56 * 1024 * 1024

_NT = (((1,), (1,)), ((), ()))
_TN_T = (((0,), (1,)), ((), ()))


def _rms(x, g):
    ms = jnp.mean(x * x, axis=-1, keepdims=True)
    return x * lax.rsqrt(ms + EPS) * g


def _rope(slab, cos, sin):
    return slab * cos + pltpu.roll(slab, LANES // 2, axis=1) * sin


def _rope_t(x1, x2, c, s):
    return x1 * c - x2 * s, x2 * c + x1 * s


def _rope_table_kernel(pos_ref, inv_ref, ct_ref, st_ref, cos_ref, sin_ref):
    ang = inv_ref[...] * pos_ref[...].astype(F32)
    ct = jnp.cos(ang)
    st = jnp.sin(ang)
    ct_ref[...] = ct
    st_ref[...] = st
    c = ct.T
    s = st.T
    half, quart = LANES // 2, LANES // 4
    lane = lax.broadcasted_iota(jnp.int32, c.shape, 1)
    low = lane < half
    c_sw = pltpu.roll(c, half, axis=1)
    s_sw = pltpu.roll(s, half, axis=1)
    cos_ref[0] = jnp.where(low, c, c_sw)
    sin_ref[0] = jnp.where(low, -s, s_sw)
    c_q = pltpu.roll(c_sw, quart, axis=1)
    s_q = pltpu.roll(s_sw, quart, axis=1)
    first = lane < quart
    cos_ref[1] = jnp.where(first, c_sw, c_q)
    sin_ref[1] = jnp.where(first, -s_sw, s_q)


def _rope_tables(positions, bm=1024):
    s = positions.shape[-1]
    pos = positions.reshape(1, s)
    half_mb = HEAD_DIM // 2
    half_ml = MLA_ROPE // 2
    inv_mb = ROPE_THETA ** (-(jnp.arange(half_mb, dtype=F32) * 2.0 / HEAD_DIM))
    inv_ml = ROPE_THETA ** (-(jnp.arange(half_ml, dtype=F32) * 2.0 / MLA_ROPE))
    inv = jnp.concatenate([inv_mb, inv_ml, jnp.zeros((LANES - half_mb - half_ml,), F32)]).reshape(LANES, 1)
    t_out = jax.ShapeDtypeStruct((LANES, s), F32)
    r_out = jax.ShapeDtypeStruct((2, s, LANES), F32)
    return pl.pallas_call(
        _rope_table_kernel,
        grid=(s // bm,),
        in_specs=[pl.BlockSpec((1, bm), lambda i: (0, i)),
                  pl.BlockSpec((LANES, 1), lambda i: (0, 0))],
        out_specs=[pl.BlockSpec((LANES, bm), lambda i: (0, i))] * 2
                  + [pl.BlockSpec((2, bm, LANES), lambda i: (0, i, 0))] * 2,
        out_shape=[t_out, t_out, r_out, r_out],
        name="rope_tables",
    )(pos, inv)


def _norm_kernel(x_ref, g_ref, a_ref):
    a_ref[...] = _rms(x_ref[...], g_ref[...]).astype(a_ref.dtype)


def _attn_norm(x2, g, bm=1024):
    s, d = x2.shape
    return pl.pallas_call(
        _norm_kernel,
        grid=(s // bm,),
        in_specs=[pl.BlockSpec((bm, d), lambda i: (i, 0)), pl.BlockSpec((1, d), lambda i: (0, 0))],
        out_specs=pl.BlockSpec((bm, d), lambda i: (i, 0)),
        out_shape=jax.ShapeDtypeStruct((s, d), BF16),
        compiler_params=pltpu.CompilerParams(dimension_semantics=("arbitrary",), vmem_limit_bytes=VMEM_LIMIT),
        name="attn_norm",
    )(x2, g)


def _proj_kernel(a_ref, wq_ref, wm_ref, ct_ref, st_ref, cos_ref, sin_ref,
                 qtl_ref, qtm_ref, k_ref, km_ref, vt_ref, wt_scr, *, scale_mla, scale_mb):
    j = pl.program_id(0)
    i = pl.program_id(1)
    bm = a_ref.shape[0]
    half_mb, half_ml = HEAD_DIM // 2, MLA_ROPE // 2
    hq = HEAD_DIM + MLA_ROPE
    nq = wq_ref.shape[1]

    @pl.when((i == 0) & (j < 2))
    def _():
        wt_scr[0:nq, :] = wq_ref[...].T

    @pl.when((i == 0) & ((j == 2) | (j == 4)))
    def _():
        wt_scr[...] = wm_ref[...].T

    @pl.when(j < 2)
    def _():
        t = lax.dot_general(wt_scr[0:nq, :], a_ref[...], _NT, preferred_element_type=F32)
        c = ct_ref[half_mb:half_mb + half_ml, :]
        s = st_ref[half_mb:half_mb + half_ml, :]
        for h in range(nq // hq):
            r0, o0 = h * hq, h * QK_WIDTH
            qtl_ref[o0:o0 + HEAD_DIM, :] = (t[r0:r0 + HEAD_DIM, :] * scale_mla).astype(BF16)
            x1 = t[r0 + HEAD_DIM:r0 + HEAD_DIM + half_ml, :]
            x2 = t[r0 + HEAD_DIM + half_ml:r0 + hq, :]
            o1, o2 = _rope_t(x1, x2, c, s)
            qtl_ref[o0 + HEAD_DIM:o0 + HEAD_DIM + half_ml, :] = (o1 * scale_mla).astype(BF16)
            qtl_ref[o0 + HEAD_DIM + half_ml:o0 + hq, :] = (o2 * scale_mla).astype(BF16)
            qtl_ref[o0 + hq:o0 + QK_WIDTH, :] = jnp.zeros((QK_WIDTH - hq, bm), BF16)

    @pl.when(j == 2)
    def _():
        t = lax.dot_general(wt_scr[...], a_ref[...], _NT, preferred_element_type=F32)
        c = ct_ref[0:half_mb, :]
        s = st_ref[0:half_mb, :]
        for h in range(N_HEADS):
            r0 = h * HEAD_DIM
            o1, o2 = _rope_t(t[r0:r0 + half_mb, :], t[r0 + half_mb:r0 + HEAD_DIM, :], c, s)
            qtm_ref[r0:r0 + half_mb, :] = (o1 * scale_mb).astype(BF16)
            qtm_ref[r0 + half_mb:r0 + HEAD_DIM, :] = (o2 * scale_mb).astype(BF16)

    @pl.when(j == 3)
    def _():
        acc = jnp.dot(a_ref[...], wm_ref[...], preferred_element_type=F32)
        nb = bm // MOBA_BLOCK
        for h in range(N_HEADS):
            r = _rope(acc[:, h * LANES:(h + 1) * LANES], cos_ref[0], sin_ref[0])
            k_ref[:, h * LANES:(h + 1) * LANES] = r.astype(BF16)
            km = r.reshape(nb, MOBA_BLOCK, LANES).sum(axis=1) * (1.0 / MOBA_BLOCK)
            km_ref[0, :, h * LANES:(h + 1) * LANES] = km

    @pl.when(j == 4)
    def _():
        vt = lax.dot_general(wt_scr[...], a_ref[...], _NT, preferred_element_type=F32)
        vt_ref[...] = vt.astype(BF16)


def _in_proj(a, wq, wm, ct, st, cos, sin, bm=1024):
    s, d = a.shape
    mw = N_HEADS * HEAD_DIM
    hq = HEAD_DIM + MLA_ROPE
    nq = wq.shape[1] // 2
    assert wq.shape[1] == N_HEADS * hq and wm.shape[1] == 3 * mw and nq <= mw
    nb = bm // MOBA_BLOCK
    ni = s // bm
    kern = functools.partial(_proj_kernel, scale_mla=float(hq ** -0.5 * LOG2E),
                             scale_mb=float(HEAD_DIM ** -0.5 * LOG2E))
    qw = N_HEADS * QK_WIDTH

    def rows(first, last):
        return lambda j, i: jnp.where(j < first, 0, jnp.where(j > last, ni - 1, i))

    q_i, m_i, k_i, v_i = rows(0, 1), rows(2, 2), rows(3, 3), rows(4, 4)
    return pl.pallas_call(
        kern,
        grid=(5, ni),
        in_specs=[pl.BlockSpec((bm, d), lambda j, i: (i, 0)),
                  pl.BlockSpec((d, nq), lambda j, i: (0, jnp.minimum(j, 1))),
                  pl.BlockSpec((d, mw), lambda j, i: (0, jnp.clip(j - 2, 0, 2))),
                  pl.BlockSpec((LANES, bm), lambda j, i: (0, i)),
                  pl.BlockSpec((LANES, bm), lambda j, i: (0, i)),
                  pl.BlockSpec((2, bm, LANES), lambda j, i: (0, i, 0)),
                  pl.BlockSpec((2, bm, LANES), lambda j, i: (0, i, 0))],
        out_specs=[pl.BlockSpec((qw // 2, bm), lambda j, i: (jnp.minimum(j, 1), q_i(j, i))),
                   pl.BlockSpec((mw, bm), lambda j, i: (0, m_i(j, i))),
                   pl.BlockSpec((bm, mw), lambda j, i: (k_i(j, i), 0)),
                   pl.BlockSpec((1, nb, mw), lambda j, i: (k_i(j, i), 0, 0)),
                   pl.BlockSpec((mw, bm), lambda j, i: (0, v_i(j, i)))],
        out_shape=[jax.ShapeDtypeStruct((qw, s), BF16),
                   jax.ShapeDtypeStruct((mw, s), BF16),
                   jax.ShapeDtypeStruct((s, mw), BF16),
                   jax.ShapeDtypeStruct((ni, nb, mw), F32),
                   jax.ShapeDtypeStruct((mw, s), BF16)],
        scratch_shapes=[pltpu.VMEM((mw, d), BF16)],
        compiler_params=pltpu.CompilerParams(
            dimension_semantics=("arbitrary", "arbitrary"), vmem_limit_bytes=VMEM_LIMIT),
        name="in_proj",
    )(a, wq, wm, ct, st, cos, sin)


def _mla_kv_kernel(a_ref, wc_ref, kvg_ref, wuk_ref, wuv_ref, cos_ref, sin_ref, k_ref, vt_ref):
    t = jnp.dot(a_ref[...], wc_ref[...], preferred_element_type=F32)
    cn = _rms(t[:, :KV_RANK], kvg_ref[...]).astype(BF16)
    kn = jnp.dot(cn, wuk_ref[...], preferred_element_type=F32)
    vt = lax.dot_general(wuv_ref[...], cn, _TN_T, preferred_element_type=F32)
    slab = t[:, KV_RANK:]
    quart = LANES // 4
    lane = lax.broadcasted_iota(jnp.int32, slab.shape, 1)
    partner = jnp.where(lane < quart, pltpu.roll(slab, LANES - quart, axis=1), pltpu.roll(slab, quart, axis=1))
    kpe = (slab * cos_ref[0] + partner * sin_ref[0]).astype(BF16)
    for h in range(N_HEADS):
        k_ref[:, (2 * h) * LANES:(2 * h + 1) * LANES] = kn[:, h * LANES:(h + 1) * LANES].astype(BF16)
        k_ref[:, (2 * h + 1) * LANES:(2 * h + 2) * LANES] = kpe
    vt_ref[...] = vt.astype(BF16)


def _mla_kv(a, wc, kvg, wuk, wuv, cos, sin, bm=512):
    s, d = a.shape
    full = lambda w: pl.BlockSpec(w.shape, lambda i: (0,) * w.ndim)
    return pl.pallas_call(
        _mla_kv_kernel,
        grid=(s // bm,),
        in_specs=[pl.BlockSpec((bm, d), lambda i: (i, 0)), full(wc), full(kvg), full(wuk), full(wuv),
                  pl.BlockSpec((1, bm, LANES), lambda i: (1, i, 0)),
                  pl.BlockSpec((1, bm, LANES), lambda i: (1, i, 0))],
        out_specs=[pl.BlockSpec((bm, N_HEADS * QK_WIDTH), lambda i: (i, 0)),
                   pl.BlockSpec((N_HEADS * HEAD_DIM, bm), lambda i: (0, i))],
        out_shape=[jax.ShapeDtypeStruct((s, N_HEADS * QK_WIDTH), BF16),
                   jax.ShapeDtypeStruct((N_HEADS * HEAD_DIM, s), BF16)],
        compiler_params=pltpu.CompilerParams(
            dimension_semantics=("arbitrary",), vmem_limit_bytes=VMEM_LIMIT),
        name="mla_kv",
    )(a, wc, kvg, wuk, wuv, cos, sin)


def _gate_kernel(qt_ref, k_ref, km_ref, qa_ref, ka_ref, *, nrow):
    i = pl.program_id(0)
    bm = k_ref.shape[0]
    nblk = km_ref.shape[0]
    blk_row = lax.broadcasted_iota(jnp.int32, (nrow, bm), 0)
    qblk = lax.shift_right_logical(i * bm + lax.broadcasted_iota(jnp.int32, (nrow, bm), 1), 8)
    rowf = blk_row.astype(F32)
    past = blk_row < qblk
    own = jnp.where(blk_row == qblk, 1.0, 0.0)
    zpad = jnp.zeros((nrow - nblk, LANES), F32)
    kcol = lax.broadcasted_iota(jnp.int32, (bm, LANES), 1)
    kblk = lax.shift_right_logical(i * bm + lax.broadcasted_iota(jnp.int32, (bm, LANES), 0), 8)
    onehot = jnp.where(kcol == kblk, 1.0, 0.0).astype(BF16)
    for h in range(N_HEADS):
        qt = qt_ref[h * HEAD_DIM:(h + 1) * HEAD_DIM, :]
        km = km_ref[:, h * LANES:(h + 1) * LANES]
        km = jnp.concatenate([km, zpad], axis=0).astype(BF16) if nrow > nblk else km.astype(BF16)
        gate = jnp.dot(km, qt, preferred_element_type=F32)
        g = jnp.where(past, gate, NEG)
        sel = own
        for t in range(MOBA_TOPK):
            m = jnp.max(g, axis=0, keepdims=True)
            idx = jnp.min(jnp.where(g == m, rowf, float(nrow)), axis=0, keepdims=True)
            pick = rowf == idx
            sel = jnp.where(pick, jnp.where(qblk > t, 1.0, sel), sel)
            g = jnp.where(pick, -jnp.inf, g)
        o0 = h * QK_WIDTH
        qa_ref[o0:o0 + HEAD_DIM, :] = qt
        qa_ref[o0 + HEAD_DIM:o0 + HEAD_DIM + nrow, :] = jnp.where(sel > 0.5, 0.0, NEG).astype(BF16)
        if HEAD_DIM + nrow < QK_WIDTH:
            qa_ref[o0 + HEAD_DIM + nrow:o0 + QK_WIDTH, :] = jnp.zeros((QK_WIDTH - HEAD_DIM - nrow, bm), BF16)
        ka_ref[:, (2 * h) * LANES:(2 * h + 1) * LANES] = k_ref[:, h * LANES:(h + 1) * LANES]
        ka_ref[:, (2 * h + 1) * LANES:(2 * h + 2) * LANES] = onehot


def _moba_gate(qt, k, kmean, bm=512):
    mw, s = qt.shape
    nblk = kmean.shape[0]
    nrow = -(-nblk // BF16_ROWS) * BF16_ROWS
    assert nrow <= LANES
    return pl.pallas_call(
        functools.partial(_gate_kernel, nrow=nrow),
        grid=(s // bm,),
        in_specs=[pl.BlockSpec((mw, bm), lambda i: (0, i)),
                  pl.BlockSpec((bm, mw), lambda i: (i, 0)),
                  pl.BlockSpec((nblk, mw), lambda i: (0, 0))],
        out_specs=[pl.BlockSpec((N_HEADS * QK_WIDTH, bm), lambda i: (0, i)),
                   pl.BlockSpec((bm, N_HEADS * QK_WIDTH), lambda i: (i, 0))],
        out_shape=[jax.ShapeDtypeStruct((N_HEADS * QK_WIDTH, s), BF16),
                   jax.ShapeDtypeStruct((s, N_HEADS * QK_WIDTH), BF16)],
        compiler_params=pltpu.CompilerParams(dimension_semantics=("arbitrary",)),
        name="moba_gate",
    )(qt, k, kmean)


def _flash_kernel(qt_ref, k_ref, vt_ref, o_ref, s_scr, cm_scr, m_scr, l_scr, acc_scr, *, ng, tg, tk):
    qi = pl.program_id(1)
    tq = ng * tg
    r = tk // tg
    m_scr[...] = jnp.full(m_scr.shape, M_INIT, F32)
    l_scr[...] = jnp.zeros(l_scr.shape, F32)
    acc_scr[...] = jnp.zeros(acc_scr.shape, F32)

    def start(buf, g, kj):
        k = k_ref[pl.ds(pl.multiple_of(kj * tk, tk), tk), :]
        st = jnp.dot(k, qt_ref[:, g * tg:(g + 1) * tg], preferred_element_type=F32)
        s_scr[buf] = st
        cm_scr[buf] = jnp.max(st, axis=0, keepdims=True)

    def finish(buf, g, kj, masked):
        st = s_scr[buf]
        if masked:
            key = kj * tk + lax.broadcasted_iota(jnp.int32, (tk, tg), 0)
            qry = qi * tq + g * tg + lax.broadcasted_iota(jnp.int32, (tk, tg), 1)
            st = jnp.where(key <= qry, st, NEG)
            cm = jnp.max(st, axis=0, keepdims=True)
        else:
            cm = cm_scr[buf]
        m_prev = m_scr[g]
        m_new = jnp.maximum(m_prev, cm)
        alpha = jnp.exp2(m_prev - m_new)
        p = jnp.exp2(st - m_new).astype(BF16)
        vt = vt_ref[:, pl.ds(pl.multiple_of(kj * tk, tk), tk)]
        vt1 = jnp.concatenate([vt, jnp.ones((ONES_ROWS, tk), BF16)], axis=0)
        pv = jnp.dot(vt1, p, preferred_element_type=F32)
        l_scr[g] = alpha * l_scr[g] + pv[HEAD_DIM:HEAD_DIM + 1, :]
        acc_scr[g] = alpha * acc_scr[g] + pv[:HEAD_DIM, :]
        m_scr[g] = m_new

    n_full = qi * (ng // r)
    tail = [(c, g, g < (c + 1) * r) for c in range(ng // r) for g in range(c * r, ng)]

    start(0, 0, 0)

    def chunk(kj):
        for g in range(ng):
            if g + 1 < ng:
                start((g + 1) % 2, g + 1, kj)
            else:
                start(0, 0, kj + 1)
            finish(g % 2, g, kj, False)

    unroll = 2 if (ng // r) % 2 == 0 else 1

    def body(t, c):
        for u in range(unroll):
            chunk(t * unroll + u)
        return c

    lax.fori_loop(0, n_full // unroll, body, 0)
    for i, (c, g, masked) in enumerate(tail):
        if i + 1 < len(tail):
            nc, ngp, _ = tail[i + 1]
            start((i + 1) % 2, ngp, n_full + nc)
        finish(i % 2, g, n_full + c, masked)
    for g in range(ng):
        o = acc_scr[g] / l_scr[g]
        o_ref[g * tg:(g + 1) * tg, :] = o.T.astype(o_ref.dtype)


def _flash(qt, k, vt, ng=4, tg=512, tk=512):
    s = k.shape[0]
    tq = ng * tg
    assert ng % 2 == 0 and tk % tg == 0 and tq % tk == 0
    kern = functools.partial(_flash_kernel, ng=ng, tg=tg, tk=tk)
    return pl.pallas_call(
        kern,
        grid=(N_HEADS, s // tq),
        in_specs=[pl.BlockSpec((QK_WIDTH, tq), lambda h, i: (h, i)),
                  pl.BlockSpec((s, QK_WIDTH), lambda h, i: (0, h)),
                  pl.BlockSpec((HEAD_DIM, s), lambda h, i: (h, 0))],
        out_specs=pl.BlockSpec((tq, HEAD_DIM), lambda h, i: (i, h)),
        out_shape=jax.ShapeDtypeStruct((s, N_HEADS * HEAD_DIM), BF16),
        scratch_shapes=[pltpu.VMEM((2, tk, tg), F32), pltpu.VMEM((2, 1, tg), F32),
                        pltpu.VMEM((ng, 1, tg), F32), pltpu.VMEM((ng, 1, tg), F32),
                        pltpu.VMEM((ng, HEAD_DIM, tg), F32)],
        compiler_params=pltpu.CompilerParams(
            dimension_semantics=("arbitrary", "arbitrary"), vmem_limit_bytes=VMEM_LIMIT),
        name="flash_attn",
    )(qt, k, vt)


def _wo_kernel(x_ref, a_ref, b_ref, wa_ref, wb_ref, o_ref):
    o_ref[...] = (x_ref[...]
                  + jnp.dot(a_ref[...], wa_ref[...], preferred_element_type=F32)
                  + jnp.dot(b_ref[...], wb_ref[...], preferred_element_type=F32))


def _out_proj(x2, oa, ob, wa, wb, bm=512):
    s, d = x2.shape
    ka, kb = oa.shape[1], ob.shape[1]
    return pl.pallas_call(
        _wo_kernel,
        grid=(s // bm,),
        in_specs=[pl.BlockSpec((bm, d), lambda i: (i, 0)),
                  pl.BlockSpec((bm, ka), lambda i: (i, 0)),
                  pl.BlockSpec((bm, kb), lambda i: (i, 0)),
                  pl.BlockSpec((ka, d), lambda i: (0, 0)),
                  pl.BlockSpec((kb, d), lambda i: (0, 0))],
        out_specs=pl.BlockSpec((bm, d), lambda i: (i, 0)),
        out_shape=jax.ShapeDtypeStruct((s, d), F32),
        compiler_params=pltpu.CompilerParams(
            dimension_semantics=("arbitrary",), vmem_limit_bytes=VMEM_LIMIT),
        name="out_proj",
    )(x2, oa, ob, wa, wb)


def _ffn_kernel(h_ref, g_ref, wg_ref, wu_ref, wd_ref, o_ref, f_scr):
    k = pl.program_id(1)

    @pl.when(k == 0)
    def _():
        h = h_ref[...]
        f_scr[...] = _rms(h, g_ref[...]).astype(BF16)
        o_ref[...] = h

    f = f_scr[...]
    gt = jnp.dot(f, wg_ref[...], preferred_element_type=F32)
    up = jnp.dot(f, wu_ref[...], preferred_element_type=F32)
    act = (gt * jax.nn.sigmoid(gt) * up).astype(BF16)
    nchunk = wd_ref.shape[0]
    for n0 in range(0, wd_ref.shape[1], nchunk):
        o_ref[:, n0:n0 + nchunk] += jnp.dot(act, wd_ref[:, n0:n0 + nchunk], preferred_element_type=F32)


def _ffn(h, g, wg, wu, wd, bm=1024, bf=512):
    s, d = h.shape
    f = wg.shape[1]
    return pl.pallas_call(
        _ffn_kernel,
        grid=(s // bm, f // bf),
        in_specs=[pl.BlockSpec((bm, d), lambda i, k: (i, 0)),
                  pl.BlockSpec((1, d), lambda i, k: (0, 0)),
                  pl.BlockSpec((d, bf), lambda i, k: (0, k)),
                  pl.BlockSpec((d, bf), lambda i, k: (0, k)),
                  pl.BlockSpec((bf, d), lambda i, k: (k, 0))],
        out_specs=pl.BlockSpec((bm, d), lambda i, k: (i, 0)),
        out_shape=jax.ShapeDtypeStruct((s, d), F32),
        scratch_shapes=[pltpu.VMEM((bm, d), BF16)],
        compiler_params=pltpu.CompilerParams(
            dimension_semantics=("arbitrary", "arbitrary"), vmem_limit_bytes=VMEM_LIMIT),
        name="swiglu",
    )(h, g, wg, wu, wd)


def _ple_kernel(h_ref, p_ref, gp_ref, wpg_ref, wpe_ref, gf_ref, o_ref):
    h = h_ref[...]
    r = _rms(h, gp_ref[...]).astype(BF16)
    gate = jax.nn.sigmoid(jnp.dot(r, wpg_ref[...], preferred_element_type=F32))
    pe = jnp.dot(p_ref[...].astype(BF16), wpe_ref[...], preferred_element_type=F32)
    o_ref[...] = _rms(h + gate * pe, gf_ref[...])


def _ple_final(h, p2, gp, wpg, wpe, gf, bm=512):
    s, d = h.shape
    pd = p2.shape[1]
    full = lambda a: pl.BlockSpec(a.shape, lambda i: (0,) * a.ndim)
    return pl.pallas_call(
        _ple_kernel,
        grid=(s // bm,),
        in_specs=[pl.BlockSpec((bm, d), lambda i: (i, 0)),
                  pl.BlockSpec((bm, pd), lambda i: (i, 0)),
                  full(gp), full(wpg), full(wpe), full(gf)],
        out_specs=pl.BlockSpec((bm, d), lambda i: (i, 0)),
        out_shape=jax.ShapeDtypeStruct((s, d), F32),
        compiler_params=pltpu.CompilerParams(
            dimension_semantics=("arbitrary",), vmem_limit_bytes=VMEM_LIMIT),
        name="ple_final",
    )(h, p2, gp, wpg, wpe, gf)


def kernel(x, p, positions, attn_norm, w_in, kv_norm, w_ukv, w_o, ffn_norm, w_gate, w_up, w_down,
           ple_norm, w_ple_gate, w_ple_proj, final_norm):
    b, s, d = x.shape
    assert b == 1 and p.shape[0] == 1 and s % 2048 == 0
    x2 = x.reshape(s, d)
    p2 = p.reshape(s, p.shape[-1])
    row = lambda v: v.reshape(1, -1).astype(F32)

    q_cols = N_HEADS * (HEAD_DIM + MLA_ROPE)
    c1 = q_cols + KV_RANK
    c2 = c1 + MLA_ROPE
    w_in0 = w_in[0].astype(BF16)
    w_q = w_in0[:, :q_cols]
    w_ckv = jnp.pad(w_in0[:, q_cols:c2], ((0, 0), (0, LANES - MLA_ROPE)))
    w_mb = w_in0[:, c2:]
    wukv = w_ukv[0].reshape(KV_RANK, N_HEADS, 2 * HEAD_DIM)
    w_uk = wukv[:, :, :HEAD_DIM].reshape(KV_RANK, -1).astype(BF16)
    w_uv = wukv[:, :, HEAD_DIM:].reshape(KV_RANK, -1).astype(BF16)
    mla_out = N_HEADS * HEAD_DIM
    wo_a = w_o[0][:mla_out].astype(BF16)
    wo_b = w_o[0][mla_out:].astype(BF16)

    ct, st, cos, sin = _rope_tables(positions)
    a = _attn_norm(x2, row(attn_norm[0]))
    qt_mla, qt_mb, k_mb, kmean, vt_mb = _in_proj(a, w_q, w_mb, ct, st, cos, sin)
    k_mla, vt_mla = _mla_kv(a, w_ckv, row(kv_norm[0]), w_uk, w_uv, cos, sin)

    kmean = kmean.reshape(s // MOBA_BLOCK, N_HEADS * HEAD_DIM)
    qat_mb, k_aug = _moba_gate(qt_mb, k_mb, kmean)

    out_mla = _flash(qt_mla, k_mla, vt_mla)
    out_mb = _flash(qat_mb, k_aug, vt_mb)

    h1 = _out_proj(x2, out_mla, out_mb, wo_a, wo_b)
    h2 = _ffn(h1, row(ffn_norm[0]), w_gate[0].astype(BF16), w_up[0].astype(BF16), w_down[0].astype(BF16))
    out = _ple_final(h2, p2, row(ple_norm[0]), w_ple_gate[0].astype(BF16), w_ple_proj[0].astype(BF16),
                     row(final_norm))
    return out.reshape(b, s, d)
```

```python
import functools

import jax
import jax.numpy as jnp
from jax import lax
from jax.experimental import pallas as pl
from jax.experimental.pallas import tpu as pltpu

F32 = jnp.float32
BF16 = jnp.bfloat16

EPS = 1e-6
NEG = -1e30
M_INIT = -3.0e38
ROPE_THETA = 10000.0
LOG2E = 1.4426950408889634

LANES = 128
BF16_ROWS = 16
HEAD_DIM = 128
MLA_ROPE = 64
KV_RANK = 512
N_HEADS = 8
MOBA_BLOCK = 256
MOBA_TOPK = 3
QK_WIDTH = 2 * LANES
ONES_ROWS = BF16_ROWS
VMEM_LIMIT = 56 * 1024 * 1024

_NT = (((1,), (1,)), ((), ()))
_TN_T = (((0,), (1,)), ((), ()))


def _rms(x, g):
    ms = jnp.mean(x * x, axis=-1, keepdims=True)
    return x * lax.rsqrt(ms + EPS) * g


def _rope(slab, cos, sin):
    return slab * cos + pltpu.roll(slab, LANES // 2, axis=1) * sin


def _rope_t(x1, x2, c, s):
    return x1 * c - x2 * s, x2 * c + x1 * s


def _rope_table_kernel(pos_ref, inv_ref, ct_ref, st_ref, cos_ref, sin_ref):
    ang = inv_ref[...] * pos_ref[...].astype(F32)
    ct = jnp.cos(ang)
    st = jnp.sin(ang)
    ct_ref[...] = ct
    st_ref[...] = st
    c = ct.T
    s = st.T
    half, quart = LANES // 2, LANES // 4
    lane = lax.broadcasted_iota(jnp.int32, c.shape, 1)
    low = lane < half
    c_sw = pltpu.roll(c, half, axis=1)
    s_sw = pltpu.roll(s, half, axis=1)
    cos_ref[0] = jnp.where(low, c, c_sw)
    sin_ref[0] = jnp.where(low, -s, s_sw)
    c_q = pltpu.roll(c_sw, quart, axis=1)
    s_q = pltpu.roll(s_sw, quart, axis=1)
    first = lane < quart
    cos_ref[1] = jnp.where(first, c_sw, c_q)
    sin_ref[1] = jnp.where(first, -s_sw, s_q)


def _rope_tables(positions, bm=1024):
    s = positions.shape[-1]
    pos = positions.reshape(1, s)
    half_mb = HEAD_DIM // 2
    half_ml = MLA_ROPE // 2
    inv_mb = ROPE_THETA ** (-(jnp.arange(half_mb, dtype=F32) * 2.0 / HEAD_DIM))
    inv_ml = ROPE_THETA ** (-(jnp.arange(half_ml, dtype=F32) * 2.0 / MLA_ROPE))
    inv = jnp.concatenate([inv_mb, inv_ml, jnp.zeros((LANES - half_mb - half_ml,), F32)]).reshape(LANES, 1)
    t_out = jax.ShapeDtypeStruct((LANES, s), F32)
    r_out = jax.ShapeDtypeStruct((2, s, LANES), F32)
    return pl.pallas_call(
        _rope_table_kernel,
        grid=(s // bm,),
        in_specs=[pl.BlockSpec((1, bm), lambda i: (0, i)),
                  pl.BlockSpec((LANES, 1), lambda i: (0, 0))],
        out_specs=[pl.BlockSpec((LANES, bm), lambda i: (0, i))] * 2
                  + [pl.BlockSpec((2, bm, LANES), lambda i: (0, i, 0))] * 2,
        out_shape=[t_out, t_out, r_out, r_out],
        name="rope_tables",
    )(pos, inv)


def _norm_kernel(x_ref, g_ref, a_ref):
    a_ref[...] = _rms(x_ref[...], g_ref[...]).astype(a_ref.dtype)


def _attn_norm(x2, g, bm=1024):
    s, d = x2.shape
    return pl.pallas_call(
        _norm_kernel,
        grid=(s // bm,),
        in_specs=[pl.BlockSpec((bm, d), lambda i: (i, 0)), pl.BlockSpec((1, d), lambda i: (0, 0))],
        out_specs=pl.BlockSpec((bm, d), lambda i: (i, 0)),
        out_shape=jax.ShapeDtypeStruct((s, d), BF16),
        compiler_params=pltpu.CompilerParams(dimension_semantics=("arbitrary",), vmem_limit_bytes=VMEM_LIMIT),
        name="attn_norm",
    )(x2, g)


def _proj_kernel(a_ref, wq_ref, wm_ref, ct_ref, st_ref, cos_ref, sin_ref,
                 qtl_ref, qtm_ref, k_ref, km_ref, vt_ref, wt_scr, *, scale_mla, scale_mb):
    j = pl.program_id(0)
    i = pl.program_id(1)
    bm = a_ref.shape[0]
    half_mb, half_ml = HEAD_DIM // 2, MLA_ROPE // 2
    hq = HEAD_DIM + MLA_ROPE
    nq = wq_ref.shape[1]

    @pl.when((i == 0) & (j < 2))
    def _():
        wt_scr[0:nq, :] = wq_ref[...].T

    @pl.when((i == 0) & ((j == 2) | (j == 4)))
    def _():
        wt_scr[...] = wm_ref[...].T

    @pl.when(j < 2)
    def _():
        t = lax.dot_general(wt_scr[0:nq, :], a_ref[...], _NT, preferred_element_type=F32)
        c = ct_ref[half_mb:half_mb + half_ml, :]
        s = st_ref[half_mb:half_mb + half_ml, :]
        for h in range(nq // hq):
            r0, o0 = h * hq, h * QK_WIDTH
            qtl_ref[o0:o0 + HEAD_DIM, :] = (t[r0:r0 + HEAD_DIM, :] * scale_mla).astype(BF16)
            x1 = t[r0 + HEAD_DIM:r0 + HEAD_DIM + half_ml, :]
            x2 = t[r0 + HEAD_DIM + half_ml:r0 + hq, :]
            o1, o2 = _rope_t(x1, x2, c, s)
            qtl_ref[o0 + HEAD_DIM:o0 + HEAD_DIM + half_ml, :] = (o1 * scale_mla).astype(BF16)
            qtl_ref[o0 + HEAD_DIM + half_ml:o0 + hq, :] = (o2 * scale_mla).astype(BF16)
            qtl_ref[o0 + hq:o0 + QK_WIDTH, :] = jnp.zeros((QK_WIDTH - hq, bm), BF16)

    @pl.when(j == 2)
    def _():
        t = lax.dot_general(wt_scr[...], a_ref[...], _NT, preferred_element_type=F32)
        c = ct_ref[0:half_mb, :]
        s = st_ref[0:half_mb, :]
        for h in range(N_HEADS):
            r0 = h * HEAD_DIM
            o1, o2 = _rope_t(t[r0:r0 + half_mb, :], t[r0 + half_mb:r0 + HEAD_DIM, :], c, s)
            qtm_ref[r0:r0 + half_mb, :] = (o1 * scale_mb).astype(BF16)
            qtm_ref[r0 + half_mb:r0 + HEAD_DIM, :] = (o2 * scale_mb).astype(BF16)

    @pl.when(j == 3)
    def _():
        acc = jnp.dot(a_ref[...], wm_ref[...], preferred_element_type=F32)
        nb = bm // MOBA_BLOCK
        for h in range(N_HEADS):
            r = _rope(acc[:, h * LANES:(h + 1) * LANES], cos_ref[0], sin_ref[0])
            k_ref[:, h * LANES:(h + 1) * LANES] = r.astype(BF16)
            km = r.reshape(nb, MOBA_BLOCK, LANES).sum(axis=1) * (1.0 / MOBA_BLOCK)
            km_ref[0, :, h * LANES:(h + 1) * LANES] = km

    @pl.when(j == 4)
    def _():
        vt = lax.dot_general(wt_scr[...], a_ref[...], _NT, preferred_element_type=F32)
        vt_ref[...] = vt.astype(BF16)


def _in_proj(a, wq, wm, ct, st, cos, sin, bm=1024):
    s, d = a.shape
    mw = N_HEADS * HEAD_DIM
    hq = HEAD_DIM + MLA_ROPE
    nq = wq.shape[1] // 2
    assert wq.shape[1] == N_HEADS * hq and wm.shape[1] == 3 * mw and nq <= mw
    nb = bm // MOBA_BLOCK
    ni = s // bm
    kern = functools.partial(_proj_kernel, scale_mla=float(hq ** -0.5 * LOG2E),
                             scale_mb=float(HEAD_DIM ** -0.5 * LOG2E))
    qw = N_HEADS * QK_WIDTH

    def rows(first, last):
        return lambda j, i: jnp.where(j < first, 0, jnp.where(j > last, ni - 1, i))

    q_i, m_i, k_i, v_i = rows(0, 1), rows(2, 2), rows(3, 3), rows(4, 4)
    return pl.pallas_call(
        kern,
        grid=(5, ni),
        in_specs=[pl.BlockSpec((bm, d), lambda j, i: (i, 0)),
                  pl.BlockSpec((d, nq), lambda j, i: (0, jnp.minimum(j, 1))),
                  pl.BlockSpec((d, mw), lambda j, i: (0, jnp.clip(j - 2, 0, 2))),
                  pl.BlockSpec((LANES, bm), lambda j, i: (0, i)),
                  pl.BlockSpec((LANES, bm), lambda j, i: (0, i)),
                  pl.BlockSpec((2, bm, LANES), lambda j, i: (0, i, 0)),
                  pl.BlockSpec((2, bm, LANES), lambda j, i: (0, i, 0))],
        out_specs=[pl.BlockSpec((qw // 2, bm), lambda j, i: (jnp.minimum(j, 1), q_i(j, i))),
                   pl.BlockSpec((mw, bm), lambda j, i: (0, m_i(j, i))),
                   pl.BlockSpec((bm, mw), lambda j, i: (k_i(j, i), 0)),
                   pl.BlockSpec((1, nb, mw), lambda j, i: (k_i(j, i), 0, 0)),
                   pl.BlockSpec((mw, bm), lambda j, i: (0, v_i(j, i)))],
        out_shape=[jax.ShapeDtypeStruct((qw, s), BF16),
                   jax.ShapeDtypeStruct((mw, s), BF16),
                   jax.ShapeDtypeStruct((s, mw), BF16),
                   jax.ShapeDtypeStruct((ni, nb, mw), F32),
                   jax.ShapeDtypeStruct((mw, s), BF16)],
        scratch_shapes=[pltpu.VMEM((mw, d), BF16)],
        compiler_params=pltpu.CompilerParams(
            dimension_semantics=("arbitrary", "arbitrary"), vmem_limit_bytes=VMEM_LIMIT),
        name="in_proj",
    )(a, wq, wm, ct, st, cos, sin)


def _mla_kv_kernel(a_ref, wc_ref, kvg_ref, wuk_ref, wuv_ref, cos_ref, sin_ref, k_ref, vt_ref):
    t = jnp.dot(a_ref[...], wc_ref[...], preferred_element_type=F32)
    cn = _rms(t[:, :KV_RANK], kvg_ref[...]).astype(BF16)
    kn = jnp.dot(cn, wuk_ref[...], preferred_element_type=F32)
    vt = lax.dot_general(wuv_ref[...], cn, _TN_T, preferred_element_type=F32)
    slab = t[:, KV_RANK:]
    quart = LANES // 4
    lane = lax.broadcasted_iota(jnp.int32, slab.shape, 1)
    partner = jnp.where(lane < quart, pltpu.roll(slab, LANES - quart, axis=1), pltpu.roll(slab, quart, axis=1))
    kpe = (slab * cos_ref[0] + partner * sin_ref[0]).astype(BF16)
    for h in range(N_HEADS):
        k_ref[:, (2 * h) * LANES:(2 * h + 1) * LANES] = kn[:, h * LANES:(h + 1) * LANES].astype(BF16)
        k_ref[:, (2 * h + 1) * LANES:(2 * h + 2) * LANES] = kpe
    vt_ref[...] = vt.astype(BF16)


def _mla_kv(a, wc, kvg, wuk, wuv, cos, sin, bm=512):
    s, d = a.shape
    full = lambda w: pl.BlockSpec(w.shape, lambda i: (0,) * w.ndim)
    return pl.pallas_call(
        _mla_kv_kernel,
        grid=(s // bm,),
        in_specs=[pl.BlockSpec((bm, d), lambda i: (i, 0)), full(wc), full(kvg), full(wuk), full(wuv),
                  pl.BlockSpec((1, bm, LANES), lambda i: (1, i, 0)),
                  pl.BlockSpec((1, bm, LANES), lambda i: (1, i, 0))],
        out_specs=[pl.BlockSpec((bm, N_HEADS * QK_WIDTH), lambda i: (i, 0)),
                   pl.BlockSpec((N_HEADS * HEAD_DIM, bm), lambda i: (0, i))],
        out_shape=[jax.ShapeDtypeStruct((s, N_HEADS * QK_WIDTH), BF16),
                   jax.ShapeDtypeStruct((N_HEADS * HEAD_DIM, s), BF16)],
        compiler_params=pltpu.CompilerParams(
            dimension_semantics=("arbitrary",), vmem_limit_bytes=VMEM_LIMIT),
        name="mla_kv",
    )(a, wc, kvg, wuk, wuv, cos, sin)


def _gate_kernel(qt_ref, k_ref, km_ref, qa_ref, ka_ref, *, nrow):
    i = pl.program_id(0)
    bm = k_ref.shape[0]
    nblk = km_ref.shape[0]
    blk_row = lax.broadcasted_iota(jnp.int32, (nrow, bm), 0)
    qblk = lax.shift_right_logical(i * bm + lax.broadcasted_iota(jnp.int32, (nrow, bm), 1), 8)
    rowf = blk_row.astype(F32)
    past = blk_row < qblk
    own = jnp.where(blk_row == qblk, 1.0, 0.0)
    zpad = jnp.zeros((nrow - nblk, LANES), F32)
    kcol = lax.broadcasted_iota(jnp.int32, (bm, LANES), 1)
    kblk = lax.shift_right_logical(i * bm + lax.broadcasted_iota(jnp.int32, (bm, LANES), 0), 8)
    onehot = jnp.where(kcol == kblk, 1.0, 0.0).astype(BF16)
    for h in range(N_HEADS):
        qt = qt_ref[h * HEAD_DIM:(h + 1) * HEAD_DIM, :]
        km = km_ref[:, h * LANES:(h + 1) * LANES]
        km = jnp.concatenate([km, zpad], axis=0).astype(BF16) if nrow > nblk else km.astype(BF16)
        gate = jnp.dot(km, qt, preferred_element_type=F32)
        g = jnp.where(past, gate, NEG)
        sel = own
        for t in range(MOBA_TOPK):
            m = jnp.max(g, axis=0, keepdims=True)
            idx = jnp.min(jnp.where(g == m, rowf, float(nrow)), axis=0, keepdims=True)
            pick = rowf == idx
            sel = jnp.where(pick, jnp.where(qblk > t, 1.0, sel), sel)
            g = jnp.where(pick, -jnp.inf, g)
        o0 = h * QK_WIDTH
        qa_ref[o0:o0 + HEAD_DIM, :] = qt
        qa_ref[o0 + HEAD_DIM:o0 + HEAD_DIM + nrow, :] = jnp.where(sel > 0.5, 0.0, NEG).astype(BF16)
        if HEAD_DIM + nrow < QK_WIDTH:
            qa_ref[o0 + HEAD_DIM + nrow:o0 + QK_WIDTH, :] = jnp.zeros((QK_WIDTH - HEAD_DIM - nrow, bm), BF16)
        ka_ref[:, (2 * h) * LANES:(2 * h + 1) * LANES] = k_ref[:, h * LANES:(h + 1) * LANES]
        ka_ref[:, (2 * h + 1) * LANES:(2 * h + 2) * LANES] = onehot


def _moba_gate(qt, k, kmean, bm=512):
    mw, s = qt.shape
    nblk = kmean.shape[0]
    nrow = -(-nblk // BF16_ROWS) * BF16_ROWS
    assert nrow <= LANES
    return pl.pallas_call(
        functools.partial(_gate_kernel, nrow=nrow),
        grid=(s // bm,),
        in_specs=[pl.BlockSpec((mw, bm), lambda i: (0, i)),
                  pl.BlockSpec((bm, mw), lambda i: (i, 0)),
                  pl.BlockSpec((nblk, mw), lambda i: (0, 0))],
        out_specs=[pl.BlockSpec((N_HEADS * QK_WIDTH, bm), lambda i: (0, i)),
                   pl.BlockSpec((bm, N_HEADS * QK_WIDTH), lambda i: (i, 0))],
        out_shape=[jax.ShapeDtypeStruct((N_HEADS * QK_WIDTH, s), BF16),
                   jax.ShapeDtypeStruct((s, N_HEADS * QK_WIDTH), BF16)],
        compiler_params=pltpu.CompilerParams(dimension_semantics=("arbitrary",)),
        name="moba_gate",
    )(qt, k, kmean)


def _flash_kernel(qt_ref, k_ref, vt_ref, o_ref, s_scr, cm_scr, m_scr, l_scr, acc_scr, *, ng, tg, tk):
    qi = pl.program_id(1)
    r = tk // tg
    m_scr[...] = jnp.full(m_scr.shape, M_INIT, F32)
    l_scr[...] = jnp.zeros(l_scr.shape, F32)
    acc_scr[...] = jnp.zeros(acc_scr.shape, F32)

    def start(buf, g, kj, need_cm=True):
        k = k_ref[pl.ds(pl.multiple_of(kj * tk, tk), tk), :]
        st = jnp.dot(k, qt_ref[:, g * tg:(g + 1) * tg], preferred_element_type=F32)
        s_scr[buf] = st
        if need_cm:
            cm_scr[buf] = jnp.max(st, axis=0, keepdims=True)

    def finish(buf, g, kj, diag_off):
        st = s_scr[buf]
        if diag_off is not None:
            key = lax.broadcasted_iota(jnp.int32, (tk, tg), 0)
            qry = lax.broadcasted_iota(jnp.int32, (tk, tg), 1) + diag_off
            st = jnp.where(key <= qry, st, NEG)
            cm = jnp.max(st, axis=0, keepdims=True)
        else:
            cm = cm_scr[buf]
        m_prev = m_scr[g]
        m_new = jnp.maximum(m_prev, cm)
        alpha = jnp.exp2(m_prev - m_new)
        p = jnp.exp2(st - m_new).astype(BF16)
        vt = vt_ref[:, pl.ds(pl.multiple_of(kj * tk, tk), tk)]
        vt1 = jnp.concatenate([vt, jnp.ones((ONES_ROWS, tk), BF16)], axis=0)
        pv = jnp.dot(vt1, p, preferred_element_type=F32)
        l_scr[g] = alpha * l_scr[g] + pv[HEAD_DIM:HEAD_DIM + 1, :]
        acc_scr[g] = alpha * acc_scr[g] + pv[:HEAD_DIM, :]
        m_scr[g] = m_new

    n_full = qi * (ng // r)
    tail = [(c, g, (g - c * r) * tg if g < (c + 1) * r else None)
            for c in range(ng // r) for g in range(c * r, ng)]

    start(0, 0, 0)

    def chunk(kj):
        for g in range(ng):
            if g + 1 < ng:
                start((g + 1) % 2, g + 1, kj)
            else:
                start(0, 0, kj + 1)
            finish(g % 2, g, kj, None)

    unroll = 2 if (ng // r) % 2 == 0 else 1

    def body(t, c):
        for u in range(unroll):
            chunk(t * unroll + u)
        return c

    lax.fori_loop(0, n_full // unroll, body, 0)
    for i, (c, g, diag_off) in enumerate(tail):
        if i + 1 < len(tail):
            nc, ngp, noff = tail[i + 1]
            start((i + 1) % 2, ngp, n_full + nc, need_cm=noff is None)
        finish(i % 2, g, n_full + c, diag_off)
    for g in range(ng):
        o = acc_scr[g] / l_scr[g]
        o_ref[g * tg:(g + 1) * tg, :] = o.T.astype(o_ref.dtype)


def _flash(qt, k, vt, ng=4, tg=512, tk=512):
    s = k.shape[0]
    tq = ng * tg
    assert ng % 2 == 0 and tk % tg == 0 and tq % tk == 0
    kern = functools.partial(_flash_kernel, ng=ng, tg=tg, tk=tk)
    return pl.pallas_call(
        kern,
        grid=(N_HEADS, s // tq),
        in_specs=[pl.BlockSpec((QK_WIDTH, tq), lambda h, i: (h, i)),
                  pl.BlockSpec((s, QK_WIDTH), lambda h, i: (0, h)),
                  pl.BlockSpec((HEAD_DIM, s), lambda h, i: (h, 0))],
        out_specs=pl.BlockSpec((tq, HEAD_DIM), lambda h, i: (i, h)),
        out_shape=jax.ShapeDtypeStruct((s, N_HEADS * HEAD_DIM), BF16),
        scratch_shapes=[pltpu.VMEM((2, tk, tg), F32), pltpu.VMEM((2, 1, tg), F32),
                        pltpu.VMEM((ng, 1, tg), F32), pltpu.VMEM((ng, 1, tg), F32),
                        pltpu.VMEM((ng, HEAD_DIM, tg), F32)],
        compiler_params=pltpu.CompilerParams(
            dimension_semantics=("arbitrary", "arbitrary"), vmem_limit_bytes=VMEM_LIMIT),
        name="flash_attn",
    )(qt, k, vt)


def _wo_kernel(x_ref, a_ref, b_ref, wa_ref, wb_ref, o_ref):
    o_ref[...] = (x_ref[...]
                  + jnp.dot(a_ref[...], wa_ref[...], preferred_element_type=F32)
                  + jnp.dot(b_ref[...], wb_ref[...], preferred_element_type=F32))


def _out_proj(x2, oa, ob, wa, wb, bm=512):
    s, d = x2.shape
    ka, kb = oa.shape[1], ob.shape[1]
    return pl.pallas_call(
        _wo_kernel,
        grid=(s // bm,),
        in_specs=[pl.BlockSpec((bm, d), lambda i: (i, 0)),
                  pl.BlockSpec((bm, ka), lambda i: (i, 0)),
                  pl.BlockSpec((bm, kb), lambda i: (i, 0)),
                  pl.BlockSpec((ka, d), lambda i: (0, 0)),
                  pl.BlockSpec((kb, d), lambda i: (0, 0))],
        out_specs=pl.BlockSpec((bm, d), lambda i: (i, 0)),
        out_shape=jax.ShapeDtypeStruct((s, d), F32),
        compiler_params=pltpu.CompilerParams(
            dimension_semantics=("arbitrary",), vmem_limit_bytes=VMEM_LIMIT),
        name="out_proj",
    )(x2, oa, ob, wa, wb)


def _ffn_kernel(h_ref, g_ref, wg_ref, wu_ref, wd_ref, o_ref, f_scr):
    k = pl.program_id(1)

    @pl.when(k == 0)
    def _():
        h = h_ref[...]
        f_scr[...] = _rms(h, g_ref[...]).astype(BF16)
        o_ref[...] = h

    f = f_scr[...]
    gt = jnp.dot(f, wg_ref[...], preferred_element_type=F32)
    up = jnp.dot(f, wu_ref[...], preferred_element_type=F32)
    act = (gt * jax.nn.sigmoid(gt) * up).astype(BF16)
    nchunk = wd_ref.shape[0]
    for n0 in range(0, wd_ref.shape[1], nchunk):
        o_ref[:, n0:n0 + nchunk] += jnp.dot(act, wd_ref[:, n0:n0 + nchunk], preferred_element_type=F32)


def _ffn(h, g, wg, wu, wd, bm=1024, bf=512):
    s, d = h.shape
    f = wg.shape[1]
    return pl.pallas_call(
        _ffn_kernel,
        grid=(s // bm, f // bf),
        in_specs=[pl.BlockSpec((bm, d), lambda i, k: (i, 0)),
                  pl.BlockSpec((1, d), lambda i, k: (0, 0)),
                  pl.BlockSpec((d, bf), lambda i, k: (0, k)),
                  pl.BlockSpec((d, bf), lambda i, k: (0, k)),
                  pl.BlockSpec((bf, d), lambda i, k: (k, 0))],
        out_specs=pl.BlockSpec((bm, d), lambda i, k: (i, 0)),
        out_shape=jax.ShapeDtypeStruct((s, d), F32),
        scratch_shapes=[pltpu.VMEM((bm, d), BF16)],
        compiler_params=pltpu.CompilerParams(
            dimension_semantics=("arbitrary", "arbitrary"), vmem_limit_bytes=VMEM_LIMIT),
        name="swiglu",
    )(h, g, wg, wu, wd)


def _ple_kernel(h_ref, p_ref, gp_ref, wpg_ref, wpe_ref, gf_ref, o_ref):
    h = h_ref[...]
    r = _rms(h, gp_ref[...]).astype(BF16)
    gate = jax.nn.sigmoid(jnp.dot(r, wpg_ref[...], preferred_element_type=F32))
    pe = jnp.dot(p_ref[...].astype(BF16), wpe_ref[...], preferred_element_type=F32)
    o_ref[...] = _rms(h + gate * pe, gf_ref[...])


def _ple_final(h, p2, gp, wpg, wpe, gf, bm=512):
    s, d = h.shape
    pd = p2.shape[1]
    full = lambda a: pl.BlockSpec(a.shape, lambda i: (0,) * a.ndim)
    return pl.pallas_call(
        _ple_kernel,
        grid=(s // bm,),
        in_specs=[pl.BlockSpec((bm, d), lambda i: (i, 0)),
                  pl.BlockSpec((bm, pd), lambda i: (i, 0)),
                  full(gp), full(wpg), full(wpe), full(gf)],
        out_specs=pl.BlockSpec((bm, d), lambda i: (i, 0)),
        out_shape=jax.ShapeDtypeStruct((s, d), F32),
        compiler_params=pltpu.CompilerParams(
            dimension_semantics=("arbitrary",), vmem_limit_bytes=VMEM_LIMIT),
        name="ple_final",
    )(h, p2, gp, wpg, wpe, gf)


def kernel(x, p, positions, attn_norm, w_in, kv_norm, w_ukv, w_o, ffn_norm, w_gate, w_up, w_down,
           ple_norm, w_ple_gate, w_ple_proj, final_norm):
    b, s, d = x.shape
    assert b == 1 and p.shape[0] == 1 and s % 2048 == 0
    x2 = x.reshape(s, d)
    p2 = p.reshape(s, p.shape[-1])
    row = lambda v: v.reshape(1, -1).astype(F32)

    q_cols = N_HEADS * (HEAD_DIM + MLA_ROPE)
    c1 = q_cols + KV_RANK
    c2 = c1 + MLA_ROPE
    w_in0 = w_in[0].astype(BF16)
    w_q = w_in0[:, :q_cols]
    w_ckv = jnp.pad(w_in0[:, q_cols:c2], ((0, 0), (0, LANES - MLA_ROPE)))
    w_mb = w_in0[:, c2:]
    wukv = w_ukv[0].reshape(KV_RANK, N_HEADS, 2 * HEAD_DIM)
    w_uk = wukv[:, :, :HEAD_DIM].reshape(KV_RANK, -1).astype(BF16)
    w_uv = wukv[:, :, HEAD_DIM:].reshape(KV_RANK, -1).astype(BF16)
    mla_out = N_HEADS * HEAD_DIM
    wo_a = w_o[0][:mla_out].astype(BF16)
    wo_b = w_o[0][mla_out:].astype(BF16)

    ct, st, cos, sin = _rope_tables(positions)
    a = _attn_norm(x2, row(attn_norm[0]))
    qt_mla, qt_mb, k_mb, kmean, vt_mb = _in_proj(a, w_q, w_mb, ct, st, cos, sin)
    k_mla, vt_mla = _mla_kv(a, w_ckv, row(kv_norm[0]), w_uk, w_uv, cos, sin)

    kmean = kmean.reshape(s // MOBA_BLOCK, N_HEADS * HEAD_DIM)
    qat_mb, k_aug = _moba_gate(qt_mb, k_mb, kmean)

    out_mla = _flash(qt_mla, k_mla, vt_mla)
    out_mb = _flash(qat_mb, k_aug, vt_mb)

    h1 = _out_proj(x2, out_mla, out_mb, wo_a, wo_b)
    h2 = _ffn(h1, row(ffn_norm[0]), w_gate[0].astype(BF16), w_up[0].astype(BF16), w_down[0].astype(BF16))
    out = _ple_final(h2, p2, row(ple_norm[0]), w_ple_gate[0].astype(BF16), w_ple_proj[0].astype(BF16),
                     row(final_norm))
    return out.reshape(b, s, d)
```

```python
import functools

import jax
import jax.numpy as jnp
from jax import lax
from jax.experimental import pallas as pl
from jax.experimental.pallas import tpu as pltpu

F32 = jnp.float32
BF16 = jnp.bfloat16

EPS = 1e-6
NEG = -1e30
M_INIT = -3.0e38
ROPE_THETA = 10000.0
LOG2E = 1.4426950408889634

LANES = 128
BF16_ROWS = 16
HEAD_DIM = 128
MLA_ROPE = 64
KV_RANK = 512
N_HEADS = 8
MOBA_BLOCK = 256
MOBA_TOPK = 3
QK_WIDTH = 2 * LANES
ONES_ROWS = BF16_ROWS
VMEM_LIMIT = 56 * 1024 * 1024

_NT = (((1,), (1,)), ((), ()))


def _rms(x, g):
    ms = jnp.mean(x * x, axis=-1, keepdims=True)
    return x * lax.rsqrt(ms + EPS) * g


def _rope(slab, cos, sin):
    return slab * cos + pltpu.roll(slab, LANES // 2, axis=1) * sin


def _rope_t(x1, x2, c, s):
    return x1 * c - x2 * s, x2 * c + x1 * s


def _rope_table_kernel(pos_ref, inv_ref, ct_ref, st_ref, cos_ref, sin_ref):
    ang = inv_ref[...] * pos_ref[...].astype(F32)
    ct = jnp.cos(ang)
    st = jnp.sin(ang)
    ct_ref[...] = ct
    st_ref[...] = st
    c = ct.T
    s = st.T
    half, quart = LANES // 2, LANES // 4
    lane = lax.broadcasted_iota(jnp.int32, c.shape, 1)
    low = lane < half
    c_sw = pltpu.roll(c, half, axis=1)
    s_sw = pltpu.roll(s, half, axis=1)
    cos_ref[0] = jnp.where(low, c, c_sw)
    sin_ref[0] = jnp.where(low, -s, s_sw)
    c_q = pltpu.roll(c_sw, quart, axis=1)
    s_q = pltpu.roll(s_sw, quart, axis=1)
    first = lane < quart
    cos_ref[1] = jnp.where(first, c_sw, c_q)
    sin_ref[1] = jnp.where(first, -s_sw, s_q)


def _rope_tables(positions, bm=1024):
    s = positions.shape[-1]
    pos = positions.reshape(1, s)
    half_mb = HEAD_DIM // 2
    half_ml = MLA_ROPE // 2
    inv_mb = ROPE_THETA ** (-(jnp.arange(half_mb, dtype=F32) * 2.0 / HEAD_DIM))
    inv_ml = ROPE_THETA ** (-(jnp.arange(half_ml, dtype=F32) * 2.0 / MLA_ROPE))
    inv = jnp.concatenate([inv_mb, inv_ml, jnp.zeros((LANES - half_mb - half_ml,), F32)]).reshape(LANES, 1)
    t_out = jax.ShapeDtypeStruct((LANES, s), F32)
    r_out = jax.ShapeDtypeStruct((2, s, LANES), F32)
    return pl.pallas_call(
        _rope_table_kernel,
        grid=(s // bm,),
        in_specs=[pl.BlockSpec((1, bm), lambda i: (0, i)),
                  pl.BlockSpec((LANES, 1), lambda i: (0, 0))],
        out_specs=[pl.BlockSpec((LANES, bm), lambda i: (0, i))] * 2
                  + [pl.BlockSpec((2, bm, LANES), lambda i: (0, i, 0))] * 2,
        out_shape=[t_out, t_out, r_out, r_out],
        name="rope_tables",
    )(pos, inv)


def _proj_kernel(a_ref, wq_ref, wm_ref, ct_ref, st_ref, cos_ref, sin_ref,
                 qtl_ref, qtm_ref, k_ref, km_ref, vt_ref, wt_scr, *, scale_mla, scale_mb):
    j = pl.program_id(0)
    i = pl.program_id(1)
    bm = a_ref.shape[0]
    half_mb, half_ml = HEAD_DIM // 2, MLA_ROPE // 2
    hq = HEAD_DIM + MLA_ROPE
    nq = wq_ref.shape[1]

    @pl.when((i == 0) & (j < 2))
    def _():
        wt_scr[0:nq, :] = wq_ref[...].T

    @pl.when((i == 0) & ((j == 2) | (j == 4)))
    def _():
        wt_scr[...] = wm_ref[...].T

    @pl.when(j < 2)
    def _():
        t = lax.dot_general(wt_scr[0:nq, :], a_ref[...], _NT, preferred_element_type=F32)
        c = ct_ref[half_mb:half_mb + half_ml, :]
        s = st_ref[half_mb:half_mb + half_ml, :]
        for h in range(nq // hq):
            r0, o0 = h * hq, h * QK_WIDTH
            qtl_ref[o0:o0 + HEAD_DIM, :] = (t[r0:r0 + HEAD_DIM, :] * scale_mla).astype(BF16)
            x1 = t[r0 + HEAD_DIM:r0 + HEAD_DIM + half_ml, :]
            x2 = t[r0 + HEAD_DIM + half_ml:r0 + hq, :]
            o1, o2 = _rope_t(x1, x2, c, s)
            qtl_ref[o0 + HEAD_DIM:o0 + HEAD_DIM + half_ml, :] = (o1 * scale_mla).astype(BF16)
            qtl_ref[o0 + HEAD_DIM + half_ml:o0 + hq, :] = (o2 * scale_mla).astype(BF16)
            qtl_ref[o0 + hq:o0 + QK_WIDTH, :] = jnp.zeros((QK_WIDTH - hq, bm), BF16)

    @pl.when(j == 2)
    def _():
        t = lax.dot_general(wt_scr[...], a_ref[...], _NT, preferred_element_type=F32)
        c = ct_ref[0:half_mb, :]
        s = st_ref[0:half_mb, :]
        for h in range(N_HEADS):
            r0 = h * HEAD_DIM
            o1, o2 = _rope_t(t[r0:r0 + half_mb, :], t[r0 + half_mb:r0 + HEAD_DIM, :], c, s)
            qtm_ref[r0:r0 + half_mb, :] = (o1 * scale_mb).astype(BF16)
            qtm_ref[r0 + half_mb:r0 + HEAD_DIM, :] = (o2 * scale_mb).astype(BF16)

    @pl.when(j == 3)
    def _():
        acc = jnp.dot(a_ref[...], wm_ref[...], preferred_element_type=F32)
        nb = bm // MOBA_BLOCK
        for h in range(N_HEADS):
            r = _rope(acc[:, h * LANES:(h + 1) * LANES], cos_ref[0], sin_ref[0])
            k_ref[:, h * LANES:(h + 1) * LANES] = r.astype(BF16)
            km = r.reshape(nb, MOBA_BLOCK, LANES).sum(axis=1) * (1.0 / MOBA_BLOCK)
            km_ref[0, :, h * LANES:(h + 1) * LANES] = km

    @pl.when(j == 4)
    def _():
        vt = lax.dot_general(wt_scr[...], a_ref[...], _NT, preferred_element_type=F32)
        vt_ref[...] = vt.astype(BF16)


def _in_proj(a, wq, wm, ct, st, cos, sin, bm=1024):
    s, d = a.shape
    mw = N_HEADS * HEAD_DIM
    hq = HEAD_DIM + MLA_ROPE
    nq = wq.shape[1] // 2
    assert wq.shape[1] == N_HEADS * hq and wm.shape[1] == 3 * mw and nq <= mw
    nb = bm // MOBA_BLOCK
    ni = s // bm
    kern = functools.partial(_proj_kernel, scale_mla=float(hq ** -0.5 * LOG2E),
                             scale_mb=float(HEAD_DIM ** -0.5 * LOG2E))
    qw = N_HEADS * QK_WIDTH

    def rows(first, last):
        return lambda j, i: jnp.where(j < first, 0, jnp.where(j > last, ni - 1, i))

    q_i, m_i, k_i, v_i = rows(0, 1), rows(2, 2), rows(3, 3), rows(4, 4)
    return pl.pallas_call(
        kern,
        grid=(5, ni),
        in_specs=[pl.BlockSpec((bm, d), lambda j, i: (i, 0)),
                  pl.BlockSpec((d, nq), lambda j, i: (0, jnp.minimum(j, 1))),
                  pl.BlockSpec((d, mw), lambda j, i: (0, jnp.clip(j - 2, 0, 2))),
                  pl.BlockSpec((LANES, bm), lambda j, i: (0, i)),
                  pl.BlockSpec((LANES, bm), lambda j, i: (0, i)),
                  pl.BlockSpec((1, bm, LANES), lambda j, i: (0, i, 0)),
                  pl.BlockSpec((1, bm, LANES), lambda j, i: (0, i, 0))],
        out_specs=[pl.BlockSpec((qw // 2, bm), lambda j, i: (jnp.minimum(j, 1), q_i(j, i))),
                   pl.BlockSpec((mw, bm), lambda j, i: (0, m_i(j, i))),
                   pl.BlockSpec((bm, mw), lambda j, i: (k_i(j, i), 0)),
                   pl.BlockSpec((1, nb, mw), lambda j, i: (k_i(j, i), 0, 0)),
                   pl.BlockSpec((mw, bm), lambda j, i: (0, v_i(j, i)))],
        out_shape=[jax.ShapeDtypeStruct((qw, s), BF16),
                   jax.ShapeDtypeStruct((mw, s), BF16),
                   jax.ShapeDtypeStruct((s, mw), BF16),
                   jax.ShapeDtypeStruct((ni, nb, mw), F32),
                   jax.ShapeDtypeStruct((mw, s), BF16)],
        scratch_shapes=[pltpu.VMEM((mw, d), BF16)],
        compiler_params=pltpu.CompilerParams(
            dimension_semantics=("arbitrary", "arbitrary"), vmem_limit_bytes=VMEM_LIMIT),
        name="in_proj",
    )(a, wq, wm, ct, st, cos, sin)


def _mla_kv_kernel(x_ref, g_ref, wc_ref, kvg_ref, wuk_ref, wuv_ref, cos_ref, sin_ref,
                   a_ref, k_ref, vt_ref, wuvt_scr):
    @pl.when(pl.program_id(0) == 0)
    def _():
        wuvt_scr[...] = wuv_ref[...].T

    a = _rms(x_ref[...], g_ref[...]).astype(BF16)
    a_ref[...] = a
    t = jnp.dot(a, wc_ref[...], preferred_element_type=F32)
    cn = _rms(t[:, :KV_RANK], kvg_ref[...]).astype(BF16)
    kn = jnp.dot(cn, wuk_ref[...], preferred_element_type=F32)
    vt = lax.dot_general(wuvt_scr[...], cn, _NT, preferred_element_type=F32)
    slab = t[:, KV_RANK:]
    quart = LANES // 4
    lane = lax.broadcasted_iota(jnp.int32, slab.shape, 1)
    partner = jnp.where(lane < quart, pltpu.roll(slab, LANES - quart, axis=1), pltpu.roll(slab, quart, axis=1))
    kpe = (slab * cos_ref[0] + partner * sin_ref[0]).astype(BF16)
    for h in range(N_HEADS):
        k_ref[:, (2 * h) * LANES:(2 * h + 1) * LANES] = kn[:, h * LANES:(h + 1) * LANES].astype(BF16)
        k_ref[:, (2 * h + 1) * LANES:(2 * h + 2) * LANES] = kpe
    vt_ref[...] = vt.astype(BF16)


def _mla_kv(x2, g, wc, kvg, wuk, wuv, cos, sin, bm=512):
    s, d = x2.shape
    full = lambda w: pl.BlockSpec(w.shape, lambda i: (0,) * w.ndim)
    return pl.pallas_call(
        _mla_kv_kernel,
        grid=(s // bm,),
        in_specs=[pl.BlockSpec((bm, d), lambda i: (i, 0)), full(g), full(wc), full(kvg), full(wuk), full(wuv),
                  pl.BlockSpec((1, bm, LANES), lambda i: (1, i, 0)),
                  pl.BlockSpec((1, bm, LANES), lambda i: (1, i, 0))],
        out_specs=[pl.BlockSpec((bm, d), lambda i: (i, 0)),
                   pl.BlockSpec((bm, N_HEADS * QK_WIDTH), lambda i: (i, 0)),
                   pl.BlockSpec((N_HEADS * HEAD_DIM, bm), lambda i: (0, i))],
        out_shape=[jax.ShapeDtypeStruct((s, d), BF16),
                   jax.ShapeDtypeStruct((s, N_HEADS * QK_WIDTH), BF16),
                   jax.ShapeDtypeStruct((N_HEADS * HEAD_DIM, s), BF16)],
        scratch_shapes=[pltpu.VMEM((wuv.shape[1], wuv.shape[0]), BF16)],
        compiler_params=pltpu.CompilerParams(
            dimension_semantics=("arbitrary",), vmem_limit_bytes=VMEM_LIMIT),
        name="mla_kv",
    )(x2, g, wc, kvg, wuk, wuv, cos, sin)


def _gate_kernel(qt_ref, k_ref, km_ref, qa_ref, ka_ref, *, nrow):
    i = pl.program_id(0)
    bm = k_ref.shape[0]
    nblk = km_ref.shape[0]
    blk_row = lax.broadcasted_iota(jnp.int32, (nrow, bm), 0)
    qblk = lax.shift_right_logical(i * bm + lax.broadcasted_iota(jnp.int32, (nrow, bm), 1), 8)
    rowf = blk_row.astype(F32)
    past = blk_row < qblk
    own = jnp.where(blk_row == qblk, 1.0, 0.0)
    zpad = jnp.zeros((nrow - nblk, LANES), F32)
    kcol = lax.broadcasted_iota(jnp.int32, (bm, LANES), 1)
    kblk = lax.shift_right_logical(i * bm + lax.broadcasted_iota(jnp.int32, (bm, LANES), 0), 8)
    onehot = jnp.where(kcol == kblk, 1.0, 0.0).astype(BF16)
    for h in range(N_HEADS):
        qt = qt_ref[h * HEAD_DIM:(h + 1) * HEAD_DIM, :]
        km = km_ref[:, h * LANES:(h + 1) * LANES]
        km = jnp.concatenate([km, zpad], axis=0).astype(BF16) if nrow > nblk else km.astype(BF16)
        gate = jnp.dot(km, qt, preferred_element_type=F32)
        g = jnp.where(past, gate, NEG)
        sel = own
        for t in range(MOBA_TOPK):
            m = jnp.max(g, axis=0, keepdims=True)
            idx = jnp.min(jnp.where(g == m, rowf, float(nrow)), axis=0, keepdims=True)
            pick = rowf == idx
            sel = jnp.where(pick, jnp.where(qblk > t, 1.0, sel), sel)
            g = jnp.where(pick, -jnp.inf, g)
        o0 = h * QK_WIDTH
        qa_ref[o0:o0 + HEAD_DIM, :] = qt
        qa_ref[o0 + HEAD_DIM:o0 + HEAD_DIM + nrow, :] = jnp.where(sel > 0.5, 0.0, NEG).astype(BF16)
        if HEAD_DIM + nrow < QK_WIDTH:
            qa_ref[o0 + HEAD_DIM + nrow:o0 + QK_WIDTH, :] = jnp.zeros((QK_WIDTH - HEAD_DIM - nrow, bm), BF16)
        ka_ref[:, (2 * h) * LANES:(2 * h + 1) * LANES] = k_ref[:, h * LANES:(h + 1) * LANES]
        ka_ref[:, (2 * h + 1) * LANES:(2 * h + 2) * LANES] = onehot


def _moba_gate(qt, k, kmean, bm=512):
    mw, s = qt.shape
    nblk = kmean.shape[0]
    nrow = -(-nblk // BF16_ROWS) * BF16_ROWS
    assert nrow <= LANES
    return pl.pallas_call(
        functools.partial(_gate_kernel, nrow=nrow),
        grid=(s // bm,),
        in_specs=[pl.BlockSpec((mw, bm), lambda i: (0, i)),
                  pl.BlockSpec((bm, mw), lambda i: (i, 0)),
                  pl.BlockSpec((nblk, mw), lambda i: (0, 0))],
        out_specs=[pl.BlockSpec((N_HEADS * QK_WIDTH, bm), lambda i: (0, i)),
                   pl.BlockSpec((bm, N_HEADS * QK_WIDTH), lambda i: (i, 0))],
        out_shape=[jax.ShapeDtypeStruct((N_HEADS * QK_WIDTH, s), BF16),
                   jax.ShapeDtypeStruct((s, N_HEADS * QK_WIDTH), BF16)],
        compiler_params=pltpu.CompilerParams(dimension_semantics=("arbitrary",)),
        name="moba_gate",
    )(qt, k, kmean)


def _flash_kernel(qt_ref, k_ref, vt_ref, o_ref, s_scr, cm_scr, m_scr, l_scr, acc_scr, *, ng, tg, tk):
    qi = pl.program_id(1)
    r = tk // tg
    m_scr[...] = jnp.full(m_scr.shape, M_INIT, F32)
    l_scr[...] = jnp.zeros(l_scr.shape, F32)
    acc_scr[...] = jnp.zeros(acc_scr.shape, F32)

    def start(buf, g, kj, need_cm=True):
        k = k_ref[pl.ds(pl.multiple_of(kj * tk, tk), tk), :]
        st = jnp.dot(k, qt_ref[:, g * tg:(g + 1) * tg], preferred_element_type=F32)
        s_scr[buf] = st
        if need_cm:
            cm_scr[buf] = jnp.max(st, axis=0, keepdims=True)

    def finish(buf, g, kj, diag_off):
        st = s_scr[buf]
        if diag_off is not None:
            key = lax.broadcasted_iota(jnp.int32, (tk, tg), 0)
            qry = lax.broadcasted_iota(jnp.int32, (tk, tg), 1) + diag_off
            st = jnp.where(key <= qry, st, NEG)
            cm = jnp.max(st, axis=0, keepdims=True)
        else:
            cm = cm_scr[buf]
        m_prev = m_scr[g]
        m_new = jnp.maximum(m_prev, cm)
        alpha = jnp.exp2(m_prev - m_new)
        p = jnp.exp2(st - m_new).astype(BF16)
        vt = vt_ref[:, pl.ds(pl.multiple_of(kj * tk, tk), tk)]
        vt1 = jnp.concatenate([vt, jnp.ones((ONES_ROWS, tk), BF16)], axis=0)
        pv = jnp.dot(vt1, p, preferred_element_type=F32)
        l_scr[g] = alpha * l_scr[g] + pv[HEAD_DIM:HEAD_DIM + 1, :]
        acc_scr[g] = alpha * acc_scr[g] + pv[:HEAD_DIM, :]
        m_scr[g] = m_new

    n_full = qi * (ng // r)
    tail = [(c, g, (g - c * r) * tg if g < (c + 1) * r else None)
            for c in range(ng // r) for g in range(c * r, ng)]

    start(0, 0, 0)

    def chunk(kj):
        for g in range(ng):
            if g + 1 < ng:
                start((g + 1) % 2, g + 1, kj)
            else:
                start(0, 0, kj + 1)
            finish(g % 2, g, kj, None)

    unroll = 2 if (ng // r) % 2 == 0 else 1

    def body(t, c):
        for u in range(unroll):
            chunk(t * unroll + u)
        return c

    lax.fori_loop(0, n_full // unroll, body, 0)
    for i, (c, g, diag_off) in enumerate(tail):
        if i + 1 < len(tail):
            nc, ngp, noff = tail[i + 1]
            start((i + 1) % 2, ngp, n_full + nc, need_cm=noff is None)
        finish(i % 2, g, n_full + c, diag_off)
    for g in range(ng):
        o = acc_scr[g] / l_scr[g]
        o_ref[g * tg:(g + 1) * tg, :] = o.T.astype(o_ref.dtype)


def _flash(qt, k, vt, ng=4, tg=512, tk=512):
    s = k.shape[0]
    tq = ng * tg
    assert ng % 2 == 0 and tk % tg == 0 and tq % tk == 0
    kern = functools.partial(_flash_kernel, ng=ng, tg=tg, tk=tk)
    return pl.pallas_call(
        kern,
        grid=(N_HEADS, s // tq),
        in_specs=[pl.BlockSpec((QK_WIDTH, tq), lambda h, i: (h, i)),
                  pl.BlockSpec((s, QK_WIDTH), lambda h, i: (0, h)),
                  pl.BlockSpec((HEAD_DIM, s), lambda h, i: (h, 0))],
        out_specs=pl.BlockSpec((tq, HEAD_DIM), lambda h, i: (i, h)),
        out_shape=jax.ShapeDtypeStruct((s, N_HEADS * HEAD_DIM), BF16),
        scratch_shapes=[pltpu.VMEM((2, tk, tg), F32), pltpu.VMEM((2, 1, tg), F32),
                        pltpu.VMEM((ng, 1, tg), F32), pltpu.VMEM((ng, 1, tg), F32),
                        pltpu.VMEM((ng, HEAD_DIM, tg), F32)],
        compiler_params=pltpu.CompilerParams(
            dimension_semantics=("arbitrary", "arbitrary"), vmem_limit_bytes=VMEM_LIMIT),
        name="flash_attn",
    )(qt, k, vt)


def _wo_kernel(x_ref, a_ref, b_ref, wa_ref, wb_ref, o_ref):
    o_ref[...] = (x_ref[...]
                  + jnp.dot(a_ref[...], wa_ref[...], preferred_element_type=F32)
                  + jnp.dot(b_ref[...], wb_ref[...], preferred_element_type=F32))


def _out_proj(x2, oa, ob, wa, wb, bm=512):
    s, d = x2.shape
    ka, kb = oa.shape[1], ob.shape[1]
    return pl.pallas_call(
        _wo_kernel,
        grid=(s // bm,),
        in_specs=[pl.BlockSpec((bm, d), lambda i: (i, 0)),
                  pl.BlockSpec((bm, ka), lambda i: (i, 0)),
                  pl.BlockSpec((bm, kb), lambda i: (i, 0)),
                  pl.BlockSpec((ka, d), lambda i: (0, 0)),
                  pl.BlockSpec((kb, d), lambda i: (0, 0))],
        out_specs=pl.BlockSpec((bm, d), lambda i: (i, 0)),
        out_shape=jax.ShapeDtypeStruct((s, d), F32),
        compiler_params=pltpu.CompilerParams(
            dimension_semantics=("arbitrary",), vmem_limit_bytes=VMEM_LIMIT),
        name="out_proj",
    )(x2, oa, ob, wa, wb)


def _ffn_kernel(h_ref, g_ref, wg_ref, wu_ref, wd_ref, o_ref, f_scr):
    k = pl.program_id(1)

    @pl.when(k == 0)
    def _():
        h = h_ref[...]
        f_scr[...] = _rms(h, g_ref[...]).astype(BF16)
        o_ref[...] = h

    f = f_scr[...]
    gt = jnp.dot(f, wg_ref[...], preferred_element_type=F32)
    up = jnp.dot(f, wu_ref[...], preferred_element_type=F32)
    act = (gt * jax.nn.sigmoid(gt) * up).astype(BF16)
    nchunk = wd_ref.shape[0]
    for n0 in range(0, wd_ref.shape[1], nchunk):
        o_ref[:, n0:n0 + nchunk] += jnp.dot(act, wd_ref[:, n0:n0 + nchunk], preferred_element_type=F32)


def _ffn(h, g, wg, wu, wd, bm=1024, bf=512):
    s, d = h.shape
    f = wg.shape[1]
    return pl.pallas_call(
        _ffn_kernel,
        grid=(s // bm, f // bf),
        in_specs=[pl.BlockSpec((bm, d), lambda i, k: (i, 0)),
                  pl.BlockSpec((1, d), lambda i, k: (0, 0)),
                  pl.BlockSpec((d, bf), lambda i, k: (0, k)),
                  pl.BlockSpec((d, bf), lambda i, k: (0, k)),
                  pl.BlockSpec((bf, d), lambda i, k: (k, 0))],
        out_specs=pl.BlockSpec((bm, d), lambda i, k: (i, 0)),
        out_shape=jax.ShapeDtypeStruct((s, d), F32),
        scratch_shapes=[pltpu.VMEM((bm, d), BF16)],
        compiler_params=pltpu.CompilerParams(
            dimension_semantics=("arbitrary", "arbitrary"), vmem_limit_bytes=VMEM_LIMIT),
        name="swiglu",
    )(h, g, wg, wu, wd)


def _ple_kernel(h_ref, p_ref, gp_ref, wpg_ref, wpe_ref, gf_ref, o_ref):
    h = h_ref[...]
    r = _rms(h, gp_ref[...]).astype(BF16)
    gate = jax.nn.sigmoid(jnp.dot(r, wpg_ref[...], preferred_element_type=F32))
    pe = jnp.dot(p_ref[...].astype(BF16), wpe_ref[...], preferred_element_type=F32)
    o_ref[...] = _rms(h + gate * pe, gf_ref[...])


def _ple_final(h, p2, gp, wpg, wpe, gf, bm=512):
    s, d = h.shape
    pd = p2.shape[1]
    full = lambda a: pl.BlockSpec(a.shape, lambda i: (0,) * a.ndim)
    return pl.pallas_call(
        _ple_kernel,
        grid=(s // bm,),
        in_specs=[pl.BlockSpec((bm, d), lambda i: (i, 0)),
                  pl.BlockSpec((bm, pd), lambda i: (i, 0)),
                  full(gp), full(wpg), full(wpe), full(gf)],
        out_specs=pl.BlockSpec((bm, d), lambda i: (i, 0)),
        out_shape=jax.ShapeDtypeStruct((s, d), F32),
        compiler_params=pltpu.CompilerParams(
            dimension_semantics=("arbitrary",), vmem_limit_bytes=VMEM_LIMIT),
        name="ple_final",
    )(h, p2, gp, wpg, wpe, gf)


def kernel(x, p, positions, attn_norm, w_in, kv_norm, w_ukv, w_o, ffn_norm, w_gate, w_up, w_down,
           ple_norm, w_ple_gate, w_ple_proj, final_norm):
    b, s, d = x.shape
    assert b == 1 and p.shape[0] == 1 and s % 2048 == 0
    x2 = x.reshape(s, d)
    p2 = p.reshape(s, p.shape[-1])
    row = lambda v: v.reshape(1, -1).astype(F32)

    q_cols = N_HEADS * (HEAD_DIM + MLA_ROPE)
    c1 = q_cols + KV_RANK
    c2 = c1 + MLA_ROPE
    w_in0 = w_in[0].astype(BF16)
    w_q = w_in0[:, :q_cols]
    w_ckv = jnp.pad(w_in0[:, q_cols:c2], ((0, 0), (0, LANES - MLA_ROPE)))
    w_mb = w_in0[:, c2:]
    wukv = w_ukv[0].reshape(KV_RANK, N_HEADS, 2 * HEAD_DIM)
    w_uk = wukv[:, :, :HEAD_DIM].reshape(KV_RANK, -1).astype(BF16)
    w_uv = wukv[:, :, HEAD_DIM:].reshape(KV_RANK, -1).astype(BF16)
    mla_out = N_HEADS * HEAD_DIM
    wo_a = w_o[0][:mla_out].astype(BF16)
    wo_b = w_o[0][mla_out:].astype(BF16)

    ct, st, cos, sin = _rope_tables(positions)
    a, k_mla, vt_mla = _mla_kv(x2, row(attn_norm[0]), w_ckv, row(kv_norm[0]), w_uk, w_uv, cos, sin)
    qt_mla, qt_mb, k_mb, kmean, vt_mb = _in_proj(a, w_q, w_mb, ct, st, cos, sin)

    kmean = kmean.reshape(s // MOBA_BLOCK, N_HEADS * HEAD_DIM)
    qat_mb, k_aug = _moba_gate(qt_mb, k_mb, kmean)

    out_mla = _flash(qt_mla, k_mla, vt_mla)
    out_mb = _flash(qat_mb, k_aug, vt_mb)

    h1 = _out_proj(x2, out_mla, out_mb, wo_a, wo_b)
    h2 = _ffn(h1, row(ffn_norm[0]), w_gate[0].astype(BF16), w_up[0].astype(BF16), w_down[0].astype(BF16))
    out = _ple_final(h2, p2, row(ple_norm[0]), w_ple_gate[0].astype(BF16), w_ple_proj[0].astype(BF16),
                     row(final_norm))
    return out.reshape(b, s, d)
```

```python
import functools

import jax
import jax.numpy as jnp
from jax import lax
from jax.experimental import pallas as pl
from jax.experimental.pallas import tpu as pltpu

F32 = jnp.float32
BF16 = jnp.bfloat16

EPS = 1e-6
NEG = -1e30
M_INIT = -3.0e38
ROPE_THETA = 10000.0
LOG2E = 1.4426950408889634

LANES = 128
BF16_ROWS = 16
HEAD_DIM = 128
MLA_ROPE = 64
KV_RANK = 512
N_HEADS = 8
MOBA_BLOCK = 256
MOBA_BLOCK_LOG2 = MOBA_BLOCK.bit_length() - 1
MOBA_TOPK = 3
QK_WIDTH = 2 * LANES
ONES_ROWS = BF16_ROWS
VMEM_LIMIT = 56 * 1024 * 1024

_NT = (((1,), (1,)), ((), ()))


def _rms(x, g):
    ms = jnp.mean(x * x, axis=-1, keepdims=True)
    return x * lax.rsqrt(ms + EPS) * g


def _rope(slab, cos, sin):
    return slab * cos + pltpu.roll(slab, LANES // 2, axis=1) * sin


def _rope_t(x1, x2, c, s):
    return x1 * c - x2 * s, x2 * c + x1 * s


def _rope_table_kernel(pos_ref, inv_ref, ct_ref, st_ref, cos_ref, sin_ref):
    ang = inv_ref[...] * pos_ref[...].astype(F32)
    ct = jnp.cos(ang)
    st = jnp.sin(ang)
    ct_ref[...] = ct
    st_ref[...] = st
    c = ct.T
    s = st.T
    half, quart = LANES // 2, LANES // 4
    lane = lax.broadcasted_iota(jnp.int32, c.shape, 1)
    low = lane < half
    c_sw = pltpu.roll(c, half, axis=1)
    s_sw = pltpu.roll(s, half, axis=1)
    cos_ref[0] = jnp.where(low, c, c_sw)
    sin_ref[0] = jnp.where(low, -s, s_sw)
    c_q = pltpu.roll(c_sw, quart, axis=1)
    s_q = pltpu.roll(s_sw, quart, axis=1)
    first = lane < quart
    cos_ref[1] = jnp.where(first, c_sw, c_q)
    sin_ref[1] = jnp.where(first, -s_sw, s_q)


def _rope_tables(positions, bm=1024):
    s = positions.shape[-1]
    pos = positions.reshape(1, s)
    half_mb = HEAD_DIM // 2
    half_ml = MLA_ROPE // 2
    inv_mb = ROPE_THETA ** (-(jnp.arange(half_mb, dtype=F32) * 2.0 / HEAD_DIM))
    inv_ml = ROPE_THETA ** (-(jnp.arange(half_ml, dtype=F32) * 2.0 / MLA_ROPE))
    inv = jnp.concatenate([inv_mb, inv_ml, jnp.zeros((LANES - half_mb - half_ml,), F32)]).reshape(LANES, 1)
    t_out = jax.ShapeDtypeStruct((LANES, s), F32)
    r_out = jax.ShapeDtypeStruct((2, s, LANES), F32)
    return pl.pallas_call(
        _rope_table_kernel,
        grid=(s // bm,),
        in_specs=[pl.BlockSpec((1, bm), lambda i: (0, i)),
                  pl.BlockSpec((LANES, 1), lambda i: (0, 0))],
        out_specs=[pl.BlockSpec((LANES, bm), lambda i: (0, i))] * 2
                  + [pl.BlockSpec((2, bm, LANES), lambda i: (0, i, 0))] * 2,
        out_shape=[t_out, t_out, r_out, r_out],
        name="rope_tables",
    )(pos, inv)


def _proj_kernel(a_ref, wq_ref, wm_ref, ct_ref, st_ref, cos_ref, sin_ref,
                 qtl_ref, qtm_ref, qlo_ref, k_ref, km_ref, vt_ref, wt_scr, *, scale_mla, scale_mb):
    j = pl.program_id(0)
    i = pl.program_id(1)
    bm = a_ref.shape[0]
    half_mb, half_ml = HEAD_DIM // 2, MLA_ROPE // 2
    hq = HEAD_DIM + MLA_ROPE
    nq = wq_ref.shape[1]

    @pl.when((i == 0) & (j < 2))
    def _():
        wt_scr[0:nq, :] = wq_ref[...].T

    @pl.when((i == 0) & ((j == 2) | (j == 4)))
    def _():
        wt_scr[...] = wm_ref[...].T

    @pl.when(j < 2)
    def _():
        t = lax.dot_general(wt_scr[0:nq, :], a_ref[...], _NT, preferred_element_type=F32)
        c = ct_ref[half_mb:half_mb + half_ml, :]
        s = st_ref[half_mb:half_mb + half_ml, :]
        for h in range(nq // hq):
            r0, o0 = h * hq, h * QK_WIDTH
            qtl_ref[o0:o0 + HEAD_DIM, :] = (t[r0:r0 + HEAD_DIM, :] * scale_mla).astype(BF16)
            x1 = t[r0 + HEAD_DIM:r0 + HEAD_DIM + half_ml, :]
            x2 = t[r0 + HEAD_DIM + half_ml:r0 + hq, :]
            o1, o2 = _rope_t(x1, x2, c, s)
            qtl_ref[o0 + HEAD_DIM:o0 + HEAD_DIM + half_ml, :] = (o1 * scale_mla).astype(BF16)
            qtl_ref[o0 + HEAD_DIM + half_ml:o0 + hq, :] = (o2 * scale_mla).astype(BF16)
            qtl_ref[o0 + hq:o0 + QK_WIDTH, :] = jnp.zeros((QK_WIDTH - hq, bm), BF16)

    @pl.when(j == 2)
    def _():
        t = lax.dot_general(wt_scr[...], a_ref[...], _NT, preferred_element_type=F32)
        c = ct_ref[0:half_mb, :]
        s = st_ref[0:half_mb, :]
        for h in range(N_HEADS):
            r0 = h * HEAD_DIM
            o1, o2 = _rope_t(t[r0:r0 + half_mb, :], t[r0 + half_mb:r0 + HEAD_DIM, :], c, s)
            for off, o in ((r0, o1 * scale_mb), (r0 + half_mb, o2 * scale_mb)):
                hi = o.astype(BF16)
                qtm_ref[off:off + half_mb, :] = hi
                qlo_ref[off:off + half_mb, :] = (o - hi.astype(F32)).astype(BF16)

    @pl.when(j == 3)
    def _():
        acc = jnp.dot(a_ref[...], wm_ref[...], preferred_element_type=F32)
        nb = bm // MOBA_BLOCK
        for h in range(N_HEADS):
            r = _rope(acc[:, h * LANES:(h + 1) * LANES], cos_ref[0], sin_ref[0])
            k_ref[:, h * LANES:(h + 1) * LANES] = r.astype(BF16)
            km = r.reshape(nb, MOBA_BLOCK, LANES).sum(axis=1) * (1.0 / MOBA_BLOCK)
            km_ref[0, :, h * LANES:(h + 1) * LANES] = km

    @pl.when(j == 4)
    def _():
        vt = lax.dot_general(wt_scr[...], a_ref[...], _NT, preferred_element_type=F32)
        vt_ref[...] = vt.astype(BF16)


def _in_proj(a, wq, wm, ct, st, cos, sin, bm=1024):
    s, d = a.shape
    mw = N_HEADS * HEAD_DIM
    hq = HEAD_DIM + MLA_ROPE
    nq = wq.shape[1] // 2
    assert wq.shape[1] == N_HEADS * hq and wm.shape[1] == 3 * mw and nq <= mw
    nb = bm // MOBA_BLOCK
    ni = s // bm
    kern = functools.partial(_proj_kernel, scale_mla=float(hq ** -0.5 * LOG2E),
                             scale_mb=float(HEAD_DIM ** -0.5 * LOG2E))
    qw = N_HEADS * QK_WIDTH

    def rows(first, last):
        return lambda j, i: jnp.where(j < first, 0, jnp.where(j > last, ni - 1, i))

    q_i, m_i, k_i, v_i = rows(0, 1), rows(2, 2), rows(3, 3), rows(4, 4)
    return pl.pallas_call(
        kern,
        grid=(5, ni),
        in_specs=[pl.BlockSpec((bm, d), lambda j, i: (i, 0)),
                  pl.BlockSpec((d, nq), lambda j, i: (0, jnp.minimum(j, 1))),
                  pl.BlockSpec((d, mw), lambda j, i: (0, jnp.clip(j - 2, 0, 2))),
                  pl.BlockSpec((LANES, bm), lambda j, i: (0, i)),
                  pl.BlockSpec((LANES, bm), lambda j, i: (0, i)),
                  pl.BlockSpec((1, bm, LANES), lambda j, i: (0, i, 0)),
                  pl.BlockSpec((1, bm, LANES), lambda j, i: (0, i, 0))],
        out_specs=[pl.BlockSpec((qw // 2, bm), lambda j, i: (jnp.minimum(j, 1), q_i(j, i))),
                   pl.BlockSpec((mw, bm), lambda j, i: (0, m_i(j, i))),
                   pl.BlockSpec((mw, bm), lambda j, i: (0, m_i(j, i))),
                   pl.BlockSpec((bm, mw), lambda j, i: (k_i(j, i), 0)),
                   pl.BlockSpec((1, nb, mw), lambda j, i: (k_i(j, i), 0, 0)),
                   pl.BlockSpec((mw, bm), lambda j, i: (0, v_i(j, i)))],
        out_shape=[jax.ShapeDtypeStruct((qw, s), BF16),
                   jax.ShapeDtypeStruct((mw, s), BF16),
                   jax.ShapeDtypeStruct((mw, s), BF16),
                   jax.ShapeDtypeStruct((s, mw), BF16),
                   jax.ShapeDtypeStruct((ni, nb, mw), F32),
                   jax.ShapeDtypeStruct((mw, s), BF16)],
        scratch_shapes=[pltpu.VMEM((mw, d), BF16)],
        compiler_params=pltpu.CompilerParams(
            dimension_semantics=("arbitrary", "arbitrary"), vmem_limit_bytes=VMEM_LIMIT),
        name="in_proj",
    )(a, wq, wm, ct, st, cos, sin)


def _mla_kv_kernel(x_ref, g_ref, wc_ref, kvg_ref, wuk_ref, wuv_ref, cos_ref, sin_ref,
                   a_ref, k_ref, vt_ref, wuvt_scr):
    @pl.when(pl.program_id(0) == 0)
    def _():
        wuvt_scr[...] = wuv_ref[...].T

    a = _rms(x_ref[...], g_ref[...]).astype(BF16)
    a_ref[...] = a
    t = jnp.dot(a, wc_ref[...], preferred_element_type=F32)
    cn = _rms(t[:, :KV_RANK], kvg_ref[...]).astype(BF16)
    kn = jnp.dot(cn, wuk_ref[...], preferred_element_type=F32)
    vt = lax.dot_general(wuvt_scr[...], cn, _NT, preferred_element_type=F32)
    slab = t[:, KV_RANK:]
    quart = LANES // 4
    lane = lax.broadcasted_iota(jnp.int32, slab.shape, 1)
    partner = jnp.where(lane < quart, pltpu.roll(slab, LANES - quart, axis=1), pltpu.roll(slab, quart, axis=1))
    kpe = (slab * cos_ref[0] + partner * sin_ref[0]).astype(BF16)
    for h in range(N_HEADS):
        k_ref[:, (2 * h) * LANES:(2 * h + 1) * LANES] = kn[:, h * LANES:(h + 1) * LANES].astype(BF16)
        k_ref[:, (2 * h + 1) * LANES:(2 * h + 2) * LANES] = kpe
    vt_ref[...] = vt.astype(BF16)


def _mla_kv(x2, g, wc, kvg, wuk, wuv, cos, sin, bm=512):
    s, d = x2.shape
    full = lambda w: pl.BlockSpec(w.shape, lambda i: (0,) * w.ndim)
    return pl.pallas_call(
        _mla_kv_kernel,
        grid=(s // bm,),
        in_specs=[pl.BlockSpec((bm, d), lambda i: (i, 0)), full(g), full(wc), full(kvg), full(wuk), full(wuv),
                  pl.BlockSpec((1, bm, LANES), lambda i: (1, i, 0)),
                  pl.BlockSpec((1, bm, LANES), lambda i: (1, i, 0))],
        out_specs=[pl.BlockSpec((bm, d), lambda i: (i, 0)),
                   pl.BlockSpec((bm, N_HEADS * QK_WIDTH), lambda i: (i, 0)),
                   pl.BlockSpec((N_HEADS * HEAD_DIM, bm), lambda i: (0, i))],
        out_shape=[jax.ShapeDtypeStruct((s, d), BF16),
                   jax.ShapeDtypeStruct((s, N_HEADS * QK_WIDTH), BF16),
                   jax.ShapeDtypeStruct((N_HEADS * HEAD_DIM, s), BF16)],
        scratch_shapes=[pltpu.VMEM((wuv.shape[1], wuv.shape[0]), BF16)],
        compiler_params=pltpu.CompilerParams(
            dimension_semantics=("arbitrary",), vmem_limit_bytes=VMEM_LIMIT),
        name="mla_kv",
    )(x2, g, wc, kvg, wuk, wuv, cos, sin)


def _gate_kernel(qh_ref, ql_ref, km_ref, b_ref, *, nrow):
    i = pl.program_id(0)
    bm = qh_ref.shape[1]
    nblk = km_ref.shape[0]
    blk_row = lax.broadcasted_iota(jnp.int32, (nrow, bm), 0)
    qblk = lax.shift_right_logical(i * bm + lax.broadcasted_iota(jnp.int32, (nrow, bm), 1), MOBA_BLOCK_LOG2)
    rowf = blk_row.astype(F32)
    past = blk_row < qblk
    own = jnp.where(blk_row == qblk, 1.0, 0.0)
    zpad = jnp.zeros((nrow - nblk, LANES), F32)
    for h in range(N_HEADS):
        rows = slice(h * HEAD_DIM, (h + 1) * HEAD_DIM)
        km = km_ref[:, h * LANES:(h + 1) * LANES]
        if nrow > nblk:
            km = jnp.concatenate([km, zpad], axis=0)
        km_hi = km.astype(BF16)
        km_lo = (km - km_hi.astype(F32)).astype(BF16)
        qh = qh_ref[rows, :]
        gate = (jnp.dot(km_hi, qh, preferred_element_type=F32)
                + jnp.dot(km_hi, ql_ref[rows, :], preferred_element_type=F32)
                + jnp.dot(km_lo, qh, preferred_element_type=F32))
        g = jnp.where(past, gate, NEG)
        sel = own
        for t in range(MOBA_TOPK):
            m = jnp.max(g, axis=0, keepdims=True)
            idx = jnp.min(jnp.where(g == m, rowf, float(nrow)), axis=0, keepdims=True)
            pick = rowf == idx
            sel = jnp.where(pick, jnp.where(qblk > t, 1.0, sel), sel)
            g = jnp.where(pick, -jnp.inf, g)
        b_ref[h * nrow:(h + 1) * nrow, :] = jnp.where(sel > 0.5, 0.0, NEG).astype(BF16)


def _moba_gate(qt_hi, qt_lo, kmean, bm=512):
    mw, s = qt_hi.shape
    nblk = kmean.shape[0]
    nrow = -(-nblk // BF16_ROWS) * BF16_ROWS
    assert nrow <= LANES
    return pl.pallas_call(
        functools.partial(_gate_kernel, nrow=nrow),
        grid=(s // bm,),
        in_specs=[pl.BlockSpec((mw, bm), lambda i: (0, i)),
                  pl.BlockSpec((mw, bm), lambda i: (0, i)),
                  pl.BlockSpec((nblk, mw), lambda i: (0, 0))],
        out_specs=pl.BlockSpec((N_HEADS * nrow, bm), lambda i: (0, i)),
        out_shape=jax.ShapeDtypeStruct((N_HEADS * nrow, s), BF16),
        compiler_params=pltpu.CompilerParams(dimension_semantics=("arbitrary",)),
        name="moba_gate",
    )(qt_hi, qt_lo, kmean)


def _flash_kernel(*refs, ng, tg, tk, moba):
    if moba:
        q_ref, b_ref, k_ref, oh_ref, vt_ref, o_ref, qt_ref, s_scr, cm_scr, m_scr, l_scr, acc_scr = refs
        nrow = b_ref.shape[0]
        qt_ref[0:HEAD_DIM, :] = q_ref[...]
        qt_ref[HEAD_DIM:HEAD_DIM + nrow, :] = b_ref[...]
        if HEAD_DIM + nrow < QK_WIDTH:
            qt_ref[HEAD_DIM + nrow:QK_WIDTH, :] = jnp.zeros((QK_WIDTH - HEAD_DIM - nrow, qt_ref.shape[1]), BF16)
    else:
        qt_ref, k_ref, vt_ref, o_ref, s_scr, cm_scr, m_scr, l_scr, acc_scr = refs
    qi = pl.program_id(1)
    r = tk // tg
    m_scr[...] = jnp.full(m_scr.shape, M_INIT, F32)
    l_scr[...] = jnp.zeros(l_scr.shape, F32)
    acc_scr[...] = jnp.zeros(acc_scr.shape, F32)

    def start(buf, g, kj, need_cm=True):
        rows = pl.ds(pl.multiple_of(kj * tk, tk), tk)
        k = k_ref[rows, :]
        if moba:
            k = jnp.concatenate([k, oh_ref[rows, :]], axis=1)
        st = jnp.dot(k, qt_ref[:, g * tg:(g + 1) * tg], preferred_element_type=F32)
        s_scr[buf] = st
        if need_cm:
            cm_scr[buf] = jnp.max(st, axis=0, keepdims=True)

    def finish(buf, g, kj, diag_off):
        st = s_scr[buf]
        if diag_off is not None:
            key = lax.broadcasted_iota(jnp.int32, (tk, tg), 0)
            qry = lax.broadcasted_iota(jnp.int32, (tk, tg), 1) + diag_off
            st = jnp.where(key <= qry, st, NEG)
            cm = jnp.max(st, axis=0, keepdims=True)
        else:
            cm = cm_scr[buf]
        m_prev = m_scr[g]
        m_new = jnp.maximum(m_prev, cm)
        alpha = jnp.exp2(m_prev - m_new)
        p = jnp.exp2(st - m_new).astype(BF16)
        vt = vt_ref[:, pl.ds(pl.multiple_of(kj * tk, tk), tk)]
        vt1 = jnp.concatenate([vt, jnp.ones((ONES_ROWS, tk), BF16)], axis=0)
        pv = jnp.dot(vt1, p, preferred_element_type=F32)
        l_scr[g] = alpha * l_scr[g] + pv[HEAD_DIM:HEAD_DIM + 1, :]
        acc_scr[g] = alpha * acc_scr[g] + pv[:HEAD_DIM, :]
        m_scr[g] = m_new

    n_full = qi * (ng // r)
    tail = [(c, g, (g - c * r) * tg if g < (c + 1) * r else None)
            for c in range(ng // r) for g in range(c * r, ng)]

    start(0, 0, 0)

    def chunk(kj):
        for g in range(ng):
            if g + 1 < ng:
                start((g + 1) % 2, g + 1, kj)
            else:
                start(0, 0, kj + 1)
            finish(g % 2, g, kj, None)

    unroll = 2 if (ng // r) % 2 == 0 else 1

    def body(t, c):
        for u in range(unroll):
            chunk(t * unroll + u)
        return c

    lax.fori_loop(0, n_full // unroll, body, 0)
    for i, (c, g, diag_off) in enumerate(tail):
        if i + 1 < len(tail):
            nc, ngp, noff = tail[i + 1]
            start((i + 1) % 2, ngp, n_full + nc, need_cm=noff is None)
        finish(i % 2, g, n_full + c, diag_off)
    for g in range(ng):
        o = acc_scr[g] / l_scr[g]
        o_ref[g * tg:(g + 1) * tg, :] = o.T.astype(o_ref.dtype)


def _flash(qt, k, vt, bias=None, onehot=None, ng=4, tg=512, tk=512):
    s = k.shape[0]
    tq = ng * tg
    moba = bias is not None
    assert ng % 2 == 0 and tk % tg == 0 and tq % tk == 0
    kern = functools.partial(_flash_kernel, ng=ng, tg=tg, tk=tk, moba=moba)
    scratch = [pltpu.VMEM((2, tk, tg), F32), pltpu.VMEM((2, 1, tg), F32),
               pltpu.VMEM((ng, 1, tg), F32), pltpu.VMEM((ng, 1, tg), F32),
               pltpu.VMEM((ng, HEAD_DIM, tg), F32)]
    vt_spec = pl.BlockSpec((HEAD_DIM, s), lambda h, i: (h, 0))
    if moba:
        nrow = bias.shape[0] // N_HEADS
        in_specs = [pl.BlockSpec((HEAD_DIM, tq), lambda h, i: (h, i)),
                    pl.BlockSpec((nrow, tq), lambda h, i: (h, i)),
                    pl.BlockSpec((s, HEAD_DIM), lambda h, i: (0, h)),
                    pl.BlockSpec((s, LANES), lambda h, i: (0, 0)),
                    vt_spec]
        operands = (qt, bias, k, onehot, vt)
        scratch = [pltpu.VMEM((QK_WIDTH, tq), BF16)] + scratch
    else:
        in_specs = [pl.BlockSpec((QK_WIDTH, tq), lambda h, i: (h, i)),
                    pl.BlockSpec((s, QK_WIDTH), lambda h, i: (0, h)),
                    vt_spec]
        operands = (qt, k, vt)
    return pl.pallas_call(
        kern,
        grid=(N_HEADS, s // tq),
        in_specs=in_specs,
        out_specs=pl.BlockSpec((tq, HEAD_DIM), lambda h, i: (i, h)),
        out_shape=jax.ShapeDtypeStruct((s, N_HEADS * HEAD_DIM), BF16),
        scratch_shapes=scratch,
        compiler_params=pltpu.CompilerParams(
            dimension_semantics=("arbitrary", "arbitrary"), vmem_limit_bytes=VMEM_LIMIT),
        name="flash_attn",
    )(*operands)


def _wo_kernel(x_ref, a_ref, b_ref, wa_ref, wb_ref, o_ref):
    o_ref[...] = (x_ref[...]
                  + jnp.dot(a_ref[...], wa_ref[...], preferred_element_type=F32)
                  + jnp.dot(b_ref[...], wb_ref[...], preferred_element_type=F32))


def _out_proj(x2, oa, ob, wa, wb, bm=512):
    s, d = x2.shape
    ka, kb = oa.shape[1], ob.shape[1]
    return pl.pallas_call(
        _wo_kernel,
        grid=(s // bm,),
        in_specs=[pl.BlockSpec((bm, d), lambda i: (i, 0)),
                  pl.BlockSpec((bm, ka), lambda i: (i, 0)),
                  pl.BlockSpec((bm, kb), lambda i: (i, 0)),
                  pl.BlockSpec((ka, d), lambda i: (0, 0)),
                  pl.BlockSpec((kb, d), lambda i: (0, 0))],
        out_specs=pl.BlockSpec((bm, d), lambda i: (i, 0)),
        out_shape=jax.ShapeDtypeStruct((s, d), F32),
        compiler_params=pltpu.CompilerParams(
            dimension_semantics=("arbitrary",), vmem_limit_bytes=VMEM_LIMIT),
        name="out_proj",
    )(x2, oa, ob, wa, wb)


def _ffn_kernel(h_ref, g_ref, wg_ref, wu_ref, wd_ref, o_ref, f_scr):
    k = pl.program_id(1)

    @pl.when(k == 0)
    def _():
        h = h_ref[...]
        f_scr[...] = _rms(h, g_ref[...]).astype(BF16)
        o_ref[...] = h

    f = f_scr[...]
    gt = jnp.dot(f, wg_ref[...], preferred_element_type=F32)
    up = jnp.dot(f, wu_ref[...], preferred_element_type=F32)
    act = (gt * jax.nn.sigmoid(gt) * up).astype(BF16)
    nchunk = wd_ref.shape[0]
    for n0 in range(0, wd_ref.shape[1], nchunk):
        o_ref[:, n0:n0 + nchunk] += jnp.dot(act, wd_ref[:, n0:n0 + nchunk], preferred_element_type=F32)


def _ffn(h, g, wg, wu, wd, bm=1024, bf=512):
    s, d = h.shape
    f = wg.shape[1]
    return pl.pallas_call(
        _ffn_kernel,
        grid=(s // bm, f // bf),
        in_specs=[pl.BlockSpec((bm, d), lambda i, k: (i, 0)),
                  pl.BlockSpec((1, d), lambda i, k: (0, 0)),
                  pl.BlockSpec((d, bf), lambda i, k: (0, k)),
                  pl.BlockSpec((d, bf), lambda i, k: (0, k)),
                  pl.BlockSpec((bf, d), lambda i, k: (k, 0))],
        out_specs=pl.BlockSpec((bm, d), lambda i, k: (i, 0)),
        out_shape=jax.ShapeDtypeStruct((s, d), F32),
        scratch_shapes=[pltpu.VMEM((bm, d), BF16)],
        compiler_params=pltpu.CompilerParams(
            dimension_semantics=("arbitrary", "arbitrary"), vmem_limit_bytes=VMEM_LIMIT),
        name="swiglu",
    )(h, g, wg, wu, wd)


def _ple_kernel(h_ref, p_ref, gp_ref, wpg_ref, wpe_ref, gf_ref, o_ref):
    h = h_ref[...]
    r = _rms(h, gp_ref[...]).astype(BF16)
    gate = jax.nn.sigmoid(jnp.dot(r, wpg_ref[...], preferred_element_type=F32))
    pe = jnp.dot(p_ref[...].astype(BF16), wpe_ref[...], preferred_element_type=F32)
    o_ref[...] = _rms(h + gate * pe, gf_ref[...])


def _ple_final(h, p2, gp, wpg, wpe, gf, bm=512):
    s, d = h.shape
    pd = p2.shape[1]
    full = lambda a: pl.BlockSpec(a.shape, lambda i: (0,) * a.ndim)
    return pl.pallas_call(
        _ple_kernel,
        grid=(s // bm,),
        in_specs=[pl.BlockSpec((bm, d), lambda i: (i, 0)),
                  pl.BlockSpec((bm, pd), lambda i: (i, 0)),
                  full(gp), full(wpg), full(wpe), full(gf)],
        out_specs=pl.BlockSpec((bm, d), lambda i: (i, 0)),
        out_shape=jax.ShapeDtypeStruct((s, d), F32),
        compiler_params=pltpu.CompilerParams(
            dimension_semantics=("arbitrary",), vmem_limit_bytes=VMEM_LIMIT),
        name="ple_final",
    )(h, p2, gp, wpg, wpe, gf)


def kernel(x, p, positions, attn_norm, w_in, kv_norm, w_ukv, w_o, ffn_norm, w_gate, w_up, w_down,
           ple_norm, w_ple_gate, w_ple_proj, final_norm):
    b, s, d = x.shape
    assert b == 1 and p.shape[0] == 1 and s % 2048 == 0
    x2 = x.reshape(s, d)
    p2 = p.reshape(s, p.shape[-1])
    row = lambda v: v.reshape(1, -1).astype(F32)

    q_cols = N_HEADS * (HEAD_DIM + MLA_ROPE)
    c1 = q_cols + KV_RANK
    c2 = c1 + MLA_ROPE
    w_in0 = w_in[0].astype(BF16)
    w_q = w_in0[:, :q_cols]
    w_ckv = jnp.pad(w_in0[:, q_cols:c2], ((0, 0), (0, LANES - MLA_ROPE)))
    w_mb = w_in0[:, c2:]
    wukv = w_ukv[0].reshape(KV_RANK, N_HEADS, 2 * HEAD_DIM)
    w_uk = wukv[:, :, :HEAD_DIM].reshape(KV_RANK, -1).astype(BF16)
    w_uv = wukv[:, :, HEAD_DIM:].reshape(KV_RANK, -1).astype(BF16)
    mla_out = N_HEADS * HEAD_DIM
    wo_a = w_o[0][:mla_out].astype(BF16)
    wo_b = w_o[0][mla_out:].astype(BF16)

    ct, st, cos, sin = _rope_tables(positions)
    a, k_mla, vt_mla = _mla_kv(x2, row(attn_norm[0]), w_ckv, row(kv_norm[0]), w_uk, w_uv, cos, sin)
    qt_mla, qt_mb, qt_mb_lo, k_mb, kmean, vt_mb = _in_proj(a, w_q, w_mb, ct, st, cos, sin)

    kmean = kmean.reshape(s // MOBA_BLOCK, N_HEADS * HEAD_DIM)
    bias_mb = _moba_gate(qt_mb, qt_mb_lo, kmean)
    onehot = (jnp.arange(s, dtype=jnp.int32)[:, None] // MOBA_BLOCK
              == jnp.arange(LANES, dtype=jnp.int32)[None, :]).astype(BF16)

    out_mla = _flash(qt_mla, k_mla, vt_mla)
    out_mb = _flash(qt_mb, k_mb, vt_mb, bias=bias_mb, onehot=onehot)

    h1 = _out_proj(x2, out_mla, out_mb, wo_a, wo_b)
    h2 = _ffn(h1, row(ffn_norm[0]), w_gate[0].astype(BF16), w_up[0].astype(BF16), w_down[0].astype(BF16))
    out = _ple_final(h2, p2, row(ple_norm[0]), w_ple_gate[0].astype(BF16), w_ple_proj[0].astype(BF16),
                     row(final_norm))
    return out.reshape(b, s, d)
```

```python
import functools

import jax
import jax.numpy as jnp
import numpy as np
from jax import lax
from jax.experimental import pallas as pl
from jax.experimental.pallas import tpu as pltpu

F32 = jnp.float32
BF16 = jnp.bfloat16

EPS = 1e-6
NEG = -1e30
M_INIT = -3.0e38
ROPE_THETA = 10000.0
LOG2E = 1.4426950408889634

LANES = 128
BF16_ROWS = 16
HEAD_DIM = 128
MLA_ROPE = 64
KV_RANK = 512
N_HEADS = 8
MOBA_BLOCK = 256
MOBA_BLOCK_LOG2 = MOBA_BLOCK.bit_length() - 1
MOBA_TOPK = 3
QK_WIDTH = 2 * LANES
ONES_ROWS = BF16_ROWS
VMEM_LIMIT = 56 * 1024 * 1024

_NT = (((1,), (1,)), ((), ()))


def _rms(x, g):
    ms = jnp.mean(x * x, axis=-1, keepdims=True)
    return x * lax.rsqrt(ms + EPS) * g


def _rope(slab, cos, sin):
    return slab * cos + pltpu.roll(slab, LANES // 2, axis=1) * sin


def _rope_t(x1, x2, c, s):
    return x1 * c - x2 * s, x2 * c + x1 * s


def _rope_table_kernel(pos_ref, inv_ref, ct_ref, st_ref, cos_ref, sin_ref):
    ang = inv_ref[...] * pos_ref[...].astype(F32)
    ct = jnp.cos(ang)
    st = jnp.sin(ang)
    ct_ref[...] = ct
    st_ref[...] = st
    c = ct.T
    s = st.T
    half, quart = LANES // 2, LANES // 4
    lane = lax.broadcasted_iota(jnp.int32, c.shape, 1)
    low = lane < half
    c_sw = pltpu.roll(c, half, axis=1)
    s_sw = pltpu.roll(s, half, axis=1)
    cos_ref[0] = jnp.where(low, c, c_sw)
    sin_ref[0] = jnp.where(low, -s, s_sw)
    c_q = pltpu.roll(c_sw, quart, axis=1)
    s_q = pltpu.roll(s_sw, quart, axis=1)
    first = lane < quart
    cos_ref[1] = jnp.where(first, c_sw, c_q)
    sin_ref[1] = jnp.where(first, -s_sw, s_q)


def _rope_tables(positions, bm=1024):
    s = positions.shape[-1]
    pos = positions.reshape(1, s)
    half_mb = HEAD_DIM // 2
    half_ml = MLA_ROPE // 2
    inv_mb = ROPE_THETA ** (-(jnp.arange(half_mb, dtype=F32) * 2.0 / HEAD_DIM))
    inv_ml = ROPE_THETA ** (-(jnp.arange(half_ml, dtype=F32) * 2.0 / MLA_ROPE))
    inv = jnp.concatenate([inv_mb, inv_ml, jnp.zeros((LANES - half_mb - half_ml,), F32)]).reshape(LANES, 1)
    t_out = jax.ShapeDtypeStruct((LANES, s), F32)
    r_out = jax.ShapeDtypeStruct((2, s, LANES), F32)
    return pl.pallas_call(
        _rope_table_kernel,
        grid=(s // bm,),
        in_specs=[pl.BlockSpec((1, bm), lambda i: (0, i)),
                  pl.BlockSpec((LANES, 1), lambda i: (0, 0))],
        out_specs=[pl.BlockSpec((LANES, bm), lambda i: (0, i))] * 2
                  + [pl.BlockSpec((2, bm, LANES), lambda i: (0, i, 0))] * 2,
        out_shape=[t_out, t_out, r_out, r_out],
        name="rope_tables",
    )(pos, inv)


def _proj_kernel(a_ref, wq_ref, wm_ref, ct_ref, st_ref, cos_ref, sin_ref,
                 qtl_ref, qtm_ref, qlo_ref, k_ref, km_ref, vt_ref, wt_scr, *, scale_mla, scale_mb):
    j = pl.program_id(0)
    i = pl.program_id(1)
    bm = a_ref.shape[0]
    half_mb, half_ml = HEAD_DIM // 2, MLA_ROPE // 2
    hq = HEAD_DIM + MLA_ROPE
    nq = wq_ref.shape[1]

    @pl.when((i == 0) & (j < 2))
    def _():
        wt_scr[0:nq, :] = wq_ref[...].T

    @pl.when((i == 0) & ((j == 2) | (j == 4)))
    def _():
        wt_scr[...] = wm_ref[...].T

    @pl.when(j < 2)
    def _():
        t = lax.dot_general(wt_scr[0:nq, :], a_ref[...], _NT, preferred_element_type=F32)
        c = ct_ref[half_mb:half_mb + half_ml, :]
        s = st_ref[half_mb:half_mb + half_ml, :]
        for h in range(nq // hq):
            r0, o0 = h * hq, h * QK_WIDTH
            qtl_ref[o0:o0 + HEAD_DIM, :] = (t[r0:r0 + HEAD_DIM, :] * scale_mla).astype(BF16)
            x1 = t[r0 + HEAD_DIM:r0 + HEAD_DIM + half_ml, :]
            x2 = t[r0 + HEAD_DIM + half_ml:r0 + hq, :]
            o1, o2 = _rope_t(x1, x2, c, s)
            qtl_ref[o0 + HEAD_DIM:o0 + HEAD_DIM + half_ml, :] = (o1 * scale_mla).astype(BF16)
            qtl_ref[o0 + HEAD_DIM + half_ml:o0 + hq, :] = (o2 * scale_mla).astype(BF16)
            qtl_ref[o0 + hq:o0 + QK_WIDTH, :] = jnp.zeros((QK_WIDTH - hq, bm), BF16)

    @pl.when(j == 2)
    def _():
        t = lax.dot_general(wt_scr[...], a_ref[...], _NT, preferred_element_type=F32)
        c = ct_ref[0:half_mb, :]
        s = st_ref[0:half_mb, :]
        for h in range(N_HEADS):
            r0 = h * HEAD_DIM
            o1, o2 = _rope_t(t[r0:r0 + half_mb, :], t[r0 + half_mb:r0 + HEAD_DIM, :], c, s)
            for off, o in ((r0, o1 * scale_mb), (r0 + half_mb, o2 * scale_mb)):
                hi = o.astype(BF16)
                qtm_ref[off:off + half_mb, :] = hi
                qlo_ref[off:off + half_mb, :] = (o - hi.astype(F32)).astype(BF16)

    @pl.when(j == 3)
    def _():
        acc = jnp.dot(a_ref[...], wm_ref[...], preferred_element_type=F32)
        nb = bm // MOBA_BLOCK
        for h in range(N_HEADS):
            r = _rope(acc[:, h * LANES:(h + 1) * LANES], cos_ref[0], sin_ref[0])
            k_ref[:, h * LANES:(h + 1) * LANES] = r.astype(BF16)
            km = r.reshape(nb, MOBA_BLOCK, LANES).sum(axis=1) * (1.0 / MOBA_BLOCK)
            km_ref[0, :, h * LANES:(h + 1) * LANES] = km

    @pl.when(j == 4)
    def _():
        vt = lax.dot_general(wt_scr[...], a_ref[...], _NT, preferred_element_type=F32)
        vt_ref[...] = vt.astype(BF16)


def _in_proj(a, wq, wm, ct, st, cos, sin, bm=1024):
    s, d = a.shape
    mw = N_HEADS * HEAD_DIM
    hq = HEAD_DIM + MLA_ROPE
    nq = wq.shape[1] // 2
    assert wq.shape[1] == N_HEADS * hq and wm.shape[1] == 3 * mw and nq <= mw
    nb = bm // MOBA_BLOCK
    ni = s // bm
    kern = functools.partial(_proj_kernel, scale_mla=float(hq ** -0.5 * LOG2E),
                             scale_mb=float(HEAD_DIM ** -0.5 * LOG2E))
    qw = N_HEADS * QK_WIDTH

    def rows(first, last):
        return lambda j, i: jnp.where(j < first, 0, jnp.where(j > last, ni - 1, i))

    q_i, m_i, k_i, v_i = rows(0, 1), rows(2, 2), rows(3, 3), rows(4, 4)
    return pl.pallas_call(
        kern,
        grid=(5, ni),
        in_specs=[pl.BlockSpec((bm, d), lambda j, i: (i, 0)),
                  pl.BlockSpec((d, nq), lambda j, i: (0, jnp.minimum(j, 1))),
                  pl.BlockSpec((d, mw), lambda j, i: (0, jnp.clip(j - 2, 0, 2))),
                  pl.BlockSpec((LANES, bm), lambda j, i: (0, i)),
                  pl.BlockSpec((LANES, bm), lambda j, i: (0, i)),
                  pl.BlockSpec((1, bm, LANES), lambda j, i: (0, i, 0)),
                  pl.BlockSpec((1, bm, LANES), lambda j, i: (0, i, 0))],
        out_specs=[pl.BlockSpec((qw // 2, bm), lambda j, i: (jnp.minimum(j, 1), q_i(j, i))),
                   pl.BlockSpec((mw, bm), lambda j, i: (0, m_i(j, i))),
                   pl.BlockSpec((mw, bm), lambda j, i: (0, m_i(j, i))),
                   pl.BlockSpec((bm, mw), lambda j, i: (k_i(j, i), 0)),
                   pl.BlockSpec((1, nb, mw), lambda j, i: (k_i(j, i), 0, 0)),
                   pl.BlockSpec((mw, bm), lambda j, i: (0, v_i(j, i)))],
        out_shape=[jax.ShapeDtypeStruct((qw, s), BF16),
                   jax.ShapeDtypeStruct((mw, s), BF16),
                   jax.ShapeDtypeStruct((mw, s), BF16),
                   jax.ShapeDtypeStruct((s, mw), BF16),
                   jax.ShapeDtypeStruct((ni, nb, mw), F32),
                   jax.ShapeDtypeStruct((mw, s), BF16)],
        scratch_shapes=[pltpu.VMEM((mw, d), BF16)],
        compiler_params=pltpu.CompilerParams(
            dimension_semantics=("arbitrary", "arbitrary"), vmem_limit_bytes=VMEM_LIMIT),
        name="in_proj",
    )(a, wq, wm, ct, st, cos, sin)


def _mla_kv_kernel(x_ref, g_ref, wc_ref, kvg_ref, wuk_ref, wuv_ref, cos_ref, sin_ref,
                   a_ref, k_ref, vt_ref, wuvt_scr):
    @pl.when(pl.program_id(0) == 0)
    def _():
        wuvt_scr[...] = wuv_ref[...].T

    a = _rms(x_ref[...], g_ref[...]).astype(BF16)
    a_ref[...] = a
    t = jnp.dot(a, wc_ref[...], preferred_element_type=F32)
    cn = _rms(t[:, :KV_RANK], kvg_ref[...]).astype(BF16)
    kn = jnp.dot(cn, wuk_ref[...], preferred_element_type=F32)
    vt = lax.dot_general(wuvt_scr[...], cn, _NT, preferred_element_type=F32)
    slab = t[:, KV_RANK:]
    quart = LANES // 4
    lane = lax.broadcasted_iota(jnp.int32, slab.shape, 1)
    partner = jnp.where(lane < quart, pltpu.roll(slab, LANES - quart, axis=1), pltpu.roll(slab, quart, axis=1))
    kpe = (slab * cos_ref[0] + partner * sin_ref[0]).astype(BF16)
    for h in range(N_HEADS):
        k_ref[:, (2 * h) * LANES:(2 * h + 1) * LANES] = kn[:, h * LANES:(h + 1) * LANES].astype(BF16)
        k_ref[:, (2 * h + 1) * LANES:(2 * h + 2) * LANES] = kpe
    vt_ref[...] = vt.astype(BF16)


def _mla_kv(x2, g, wc, kvg, wuk, wuv, cos, sin, bm=512):
    s, d = x2.shape
    full = lambda w: pl.BlockSpec(w.shape, lambda i: (0,) * w.ndim)
    return pl.pallas_call(
        _mla_kv_kernel,
        grid=(s // bm,),
        in_specs=[pl.BlockSpec((bm, d), lambda i: (i, 0)), full(g), full(wc), full(kvg), full(wuk), full(wuv),
                  pl.BlockSpec((1, bm, LANES), lambda i: (1, i, 0)),
                  pl.BlockSpec((1, bm, LANES), lambda i: (1, i, 0))],
        out_specs=[pl.BlockSpec((bm, d), lambda i: (i, 0)),
                   pl.BlockSpec((bm, N_HEADS * QK_WIDTH), lambda i: (i, 0)),
                   pl.BlockSpec((N_HEADS * HEAD_DIM, bm), lambda i: (0, i))],
        out_shape=[jax.ShapeDtypeStruct((s, d), BF16),
                   jax.ShapeDtypeStruct((s, N_HEADS * QK_WIDTH), BF16),
                   jax.ShapeDtypeStruct((N_HEADS * HEAD_DIM, s), BF16)],
        scratch_shapes=[pltpu.VMEM((wuv.shape[1], wuv.shape[0]), BF16)],
        compiler_params=pltpu.CompilerParams(
            dimension_semantics=("arbitrary",), vmem_limit_bytes=VMEM_LIMIT),
        name="mla_kv",
    )(x2, g, wc, kvg, wuk, wuv, cos, sin)


def _gate_kernel(qh_ref, ql_ref, km_ref, b_ref, *, nrow):
    i = pl.program_id(0)
    bm = qh_ref.shape[1]
    nblk = km_ref.shape[0]
    blk_row = lax.broadcasted_iota(jnp.int32, (nrow, bm), 0)
    qblk = lax.shift_right_logical(i * bm + lax.broadcasted_iota(jnp.int32, (nrow, bm), 1), MOBA_BLOCK_LOG2)
    rowf = blk_row.astype(F32)
    past = blk_row < qblk
    own = jnp.where(blk_row == qblk, 1.0, 0.0)
    zpad = jnp.zeros((nrow - nblk, LANES), F32)
    for h in range(N_HEADS):
        rows = slice(h * HEAD_DIM, (h + 1) * HEAD_DIM)
        km = km_ref[:, h * LANES:(h + 1) * LANES]
        if nrow > nblk:
            km = jnp.concatenate([km, zpad], axis=0)
        km_hi = km.astype(BF16)
        km_lo = (km - km_hi.astype(F32)).astype(BF16)
        qh = qh_ref[rows, :]
        gate = (jnp.dot(km_hi, qh, preferred_element_type=F32)
                + jnp.dot(km_hi, ql_ref[rows, :], preferred_element_type=F32)
                + jnp.dot(km_lo, qh, preferred_element_type=F32))
        g = jnp.where(past, gate, NEG)
        sel = own
        for t in range(MOBA_TOPK):
            m = jnp.max(g, axis=0, keepdims=True)
            idx = jnp.min(jnp.where(g == m, rowf, float(nrow)), axis=0, keepdims=True)
            pick = rowf == idx
            sel = jnp.where(pick, jnp.where(qblk > t, 1.0, sel), sel)
            g = jnp.where(pick, -jnp.inf, g)
        b_ref[h * nrow:(h + 1) * nrow, :] = jnp.where(sel > 0.5, 0.0, NEG).astype(BF16)


def _moba_gate(qt_hi, qt_lo, kmean, bm=512):
    mw, s = qt_hi.shape
    nblk = kmean.shape[0]
    nrow = -(-nblk // BF16_ROWS) * BF16_ROWS
    assert nrow <= LANES
    return pl.pallas_call(
        functools.partial(_gate_kernel, nrow=nrow),
        grid=(s // bm,),
        in_specs=[pl.BlockSpec((mw, bm), lambda i: (0, i)),
                  pl.BlockSpec((mw, bm), lambda i: (0, i)),
                  pl.BlockSpec((nblk, mw), lambda i: (0, 0))],
        out_specs=pl.BlockSpec((N_HEADS * nrow, bm), lambda i: (0, i)),
        out_shape=jax.ShapeDtypeStruct((N_HEADS * nrow, s), BF16),
        compiler_params=pltpu.CompilerParams(dimension_semantics=("arbitrary",)),
        name="moba_gate",
    )(qt_hi, qt_lo, kmean)


def _flash_kernel(*refs, ng, tg, tk, moba):
    if moba:
        q_ref, b_ref, k_ref, oh_ref, vt_ref, o_ref, qt_ref, s_scr, cm_scr, m_scr, l_scr, acc_scr = refs
        nrow = b_ref.shape[0]
        qt_ref[0:HEAD_DIM, :] = q_ref[...]
        qt_ref[HEAD_DIM:HEAD_DIM + nrow, :] = b_ref[...]
        if HEAD_DIM + nrow < QK_WIDTH:
            qt_ref[HEAD_DIM + nrow:QK_WIDTH, :] = jnp.zeros((QK_WIDTH - HEAD_DIM - nrow, qt_ref.shape[1]), BF16)
    else:
        qt_ref, k_ref, vt_ref, o_ref, s_scr, cm_scr, m_scr, l_scr, acc_scr = refs
    qi = pl.program_id(1)
    r = tk // tg
    m_scr[...] = jnp.full(m_scr.shape, M_INIT, F32)
    l_scr[...] = jnp.zeros(l_scr.shape, F32)
    acc_scr[...] = jnp.zeros(acc_scr.shape, F32)

    def start(buf, g, kj, need_cm=True):
        rows = pl.ds(pl.multiple_of(kj * tk, tk), tk)
        k = k_ref[rows, :]
        if moba:
            k = jnp.concatenate([k, oh_ref[rows, :]], axis=1)
        st = jnp.dot(k, qt_ref[:, g * tg:(g + 1) * tg], preferred_element_type=F32)
        s_scr[buf] = st
        if need_cm:
            cm_scr[buf] = jnp.max(st, axis=0, keepdims=True)

    def finish(buf, g, kj, diag_off):
        st = s_scr[buf]
        if diag_off is not None:
            key = lax.broadcasted_iota(jnp.int32, (tk, tg), 0)
            qry = lax.broadcasted_iota(jnp.int32, (tk, tg), 1) + diag_off
            st = jnp.where(key <= qry, st, NEG)
            cm = jnp.max(st, axis=0, keepdims=True)
        else:
            cm = cm_scr[buf]
        m_prev = m_scr[g]
        m_new = jnp.maximum(m_prev, cm)
        alpha = jnp.exp2(m_prev - m_new)
        p = jnp.exp2(st - m_new).astype(BF16)
        vt = vt_ref[:, pl.ds(pl.multiple_of(kj * tk, tk), tk)]
        vt1 = jnp.concatenate([vt, jnp.ones((ONES_ROWS, tk), BF16)], axis=0)
        pv = jnp.dot(vt1, p, preferred_element_type=F32)
        l_scr[g] = alpha * l_scr[g] + pv[HEAD_DIM:HEAD_DIM + 1, :]
        acc_scr[g] = alpha * acc_scr[g] + pv[:HEAD_DIM, :]
        m_scr[g] = m_new

    n_full = qi * (ng // r)
    tail = [(c, g, (g - c * r) * tg if g < (c + 1) * r else None)
            for c in range(ng // r) for g in range(c * r, ng)]

    start(0, 0, 0)

    def chunk(kj):
        for g in range(ng):
            if g + 1 < ng:
                start((g + 1) % 2, g + 1, kj)
            else:
                start(0, 0, kj + 1)
            finish(g % 2, g, kj, None)

    unroll = 2 if (ng // r) % 2 == 0 else 1

    def body(t, c):
        for u in range(unroll):
            chunk(t * unroll + u)
        return c

    lax.fori_loop(0, n_full // unroll, body, 0)
    for i, (c, g, diag_off) in enumerate(tail):
        if i + 1 < len(tail):
            nc, ngp, noff = tail[i + 1]
            start((i + 1) % 2, ngp, n_full + nc, need_cm=noff is None)
        finish(i % 2, g, n_full + c, diag_off)
    for g in range(ng):
        o = acc_scr[g] / l_scr[g]
        o_ref[g * tg:(g + 1) * tg, :] = o.T.astype(o_ref.dtype)


def _flash(qt, k, vt, bias=None, onehot=None, ng=4, tg=512, tk=512):
    s = k.shape[0]
    tq = ng * tg
    moba = bias is not None
    assert ng % 2 == 0 and tk % tg == 0 and tq % tk == 0
    kern = functools.partial(_flash_kernel, ng=ng, tg=tg, tk=tk, moba=moba)
    scratch = [pltpu.VMEM((2, tk, tg), F32), pltpu.VMEM((2, 1, tg), F32),
               pltpu.VMEM((ng, 1, tg), F32), pltpu.VMEM((ng, 1, tg), F32),
               pltpu.VMEM((ng, HEAD_DIM, tg), F32)]
    vt_spec = pl.BlockSpec((HEAD_DIM, s), lambda h, i: (h, 0))
    if moba:
        nrow = bias.shape[0] // N_HEADS
        in_specs = [pl.BlockSpec((HEAD_DIM, tq), lambda h, i: (h, i)),
                    pl.BlockSpec((nrow, tq), lambda h, i: (h, i)),
                    pl.BlockSpec((s, HEAD_DIM), lambda h, i: (0, h)),
                    pl.BlockSpec((s, LANES), lambda h, i: (0, 0)),
                    vt_spec]
        operands = (qt, bias, k, onehot, vt)
        scratch = [pltpu.VMEM((QK_WIDTH, tq), BF16)] + scratch
    else:
        in_specs = [pl.BlockSpec((QK_WIDTH, tq), lambda h, i: (h, i)),
                    pl.BlockSpec((s, QK_WIDTH), lambda h, i: (0, h)),
                    vt_spec]
        operands = (qt, k, vt)
    return pl.pallas_call(
        kern,
        grid=(N_HEADS, s // tq),
        in_specs=in_specs,
        out_specs=pl.BlockSpec((tq, HEAD_DIM), lambda h, i: (i, h)),
        out_shape=jax.ShapeDtypeStruct((s, N_HEADS * HEAD_DIM), BF16),
        scratch_shapes=scratch,
        compiler_params=pltpu.CompilerParams(
            dimension_semantics=("arbitrary", "arbitrary"), vmem_limit_bytes=VMEM_LIMIT),
        name="flash_attn",
    )(*operands)


def _wo_kernel(x_ref, a_ref, b_ref, wa_ref, wb_ref, o_ref):
    o_ref[...] = (x_ref[...]
                  + jnp.dot(a_ref[...], wa_ref[...], preferred_element_type=F32)
                  + jnp.dot(b_ref[...], wb_ref[...], preferred_element_type=F32))


def _out_proj(x2, oa, ob, wa, wb, bm=512):
    s, d = x2.shape
    ka, kb = oa.shape[1], ob.shape[1]
    return pl.pallas_call(
        _wo_kernel,
        grid=(s // bm,),
        in_specs=[pl.BlockSpec((bm, d), lambda i: (i, 0)),
                  pl.BlockSpec((bm, ka), lambda i: (i, 0)),
                  pl.BlockSpec((bm, kb), lambda i: (i, 0)),
                  pl.BlockSpec((ka, d), lambda i: (0, 0)),
                  pl.BlockSpec((kb, d), lambda i: (0, 0))],
        out_specs=pl.BlockSpec((bm, d), lambda i: (i, 0)),
        out_shape=jax.ShapeDtypeStruct((s, d), F32),
        compiler_params=pltpu.CompilerParams(
            dimension_semantics=("arbitrary",), vmem_limit_bytes=VMEM_LIMIT),
        name="out_proj",
    )(x2, oa, ob, wa, wb)


def _ffn_kernel(h_ref, g_ref, wg_ref, wu_ref, wd_ref, o_ref, f_scr):
    k = pl.program_id(1)

    @pl.when(k == 0)
    def _():
        h = h_ref[...]
        f_scr[...] = _rms(h, g_ref[...]).astype(BF16)
        o_ref[...] = h

    f = f_scr[...]
    gt = jnp.dot(f, wg_ref[...], preferred_element_type=F32)
    up = jnp.dot(f, wu_ref[...], preferred_element_type=F32)
    act = (gt * jax.nn.sigmoid(gt) * up).astype(BF16)
    nchunk = wd_ref.shape[0]
    for n0 in range(0, wd_ref.shape[1], nchunk):
        o_ref[:, n0:n0 + nchunk] += jnp.dot(act, wd_ref[:, n0:n0 + nchunk], preferred_element_type=F32)


def _ffn(h, g, wg, wu, wd, bm=1024, bf=512):
    s, d = h.shape
    f = wg.shape[1]
    return pl.pallas_call(
        _ffn_kernel,
        grid=(s // bm, f // bf),
        in_specs=[pl.BlockSpec((bm, d), lambda i, k: (i, 0)),
                  pl.BlockSpec((1, d), lambda i, k: (0, 0)),
                  pl.BlockSpec((d, bf), lambda i, k: (0, k)),
                  pl.BlockSpec((d, bf), lambda i, k: (0, k)),
                  pl.BlockSpec((bf, d), lambda i, k: (k, 0))],
        out_specs=pl.BlockSpec((bm, d), lambda i, k: (i, 0)),
        out_shape=jax.ShapeDtypeStruct((s, d), F32),
        scratch_shapes=[pltpu.VMEM((bm, d), BF16)],
        compiler_params=pltpu.CompilerParams(
            dimension_semantics=("arbitrary", "arbitrary"), vmem_limit_bytes=VMEM_LIMIT),
        name="swiglu",
    )(h, g, wg, wu, wd)


def _ple_kernel(h_ref, p_ref, gp_ref, wpg_ref, wpe_ref, gf_ref, o_ref):
    h = h_ref[...]
    r = _rms(h, gp_ref[...]).astype(BF16)
    gate = jax.nn.sigmoid(jnp.dot(r, wpg_ref[...], preferred_element_type=F32))
    pe = jnp.dot(p_ref[...].astype(BF16), wpe_ref[...], preferred_element_type=F32)
    o_ref[...] = _rms(h + gate * pe, gf_ref[...])


def _ple_final(h, p2, gp, wpg, wpe, gf, bm=512):
    s, d = h.shape
    pd = p2.shape[1]
    full = lambda a: pl.BlockSpec(a.shape, lambda i: (0,) * a.ndim)
    return pl.pallas_call(
        _ple_kernel,
        grid=(s // bm,),
        in_specs=[pl.BlockSpec((bm, d), lambda i: (i, 0)),
                  pl.BlockSpec((bm, pd), lambda i: (i, 0)),
                  full(gp), full(wpg), full(wpe), full(gf)],
        out_specs=pl.BlockSpec((bm, d), lambda i: (i, 0)),
        out_shape=jax.ShapeDtypeStruct((s, d), F32),
        compiler_params=pltpu.CompilerParams(
            dimension_semantics=("arbitrary",), vmem_limit_bytes=VMEM_LIMIT),
        name="ple_final",
    )(h, p2, gp, wpg, wpe, gf)


def kernel(x, p, positions, attn_norm, w_in, kv_norm, w_ukv, w_o, ffn_norm, w_gate, w_up, w_down,
           ple_norm, w_ple_gate, w_ple_proj, final_norm):
    b, s, d = x.shape
    assert b == 1 and p.shape[0] == 1 and s % 2048 == 0
    x2 = x.reshape(s, d)
    p2 = p.reshape(s, p.shape[-1])
    row = lambda v: v.reshape(1, -1).astype(F32)

    q_cols = N_HEADS * (HEAD_DIM + MLA_ROPE)
    c1 = q_cols + KV_RANK
    c2 = c1 + MLA_ROPE
    w_in0 = w_in[0].astype(BF16)
    w_q = w_in0[:, :q_cols]
    w_ckv = jnp.pad(w_in0[:, q_cols:c2], ((0, 0), (0, LANES - MLA_ROPE)))
    w_mb = w_in0[:, c2:]
    wukv = w_ukv[0].reshape(KV_RANK, N_HEADS, 2 * HEAD_DIM)
    w_uk = wukv[:, :, :HEAD_DIM].reshape(KV_RANK, -1).astype(BF16)
    w_uv = wukv[:, :, HEAD_DIM:].reshape(KV_RANK, -1).astype(BF16)
    mla_out = N_HEADS * HEAD_DIM
    wo_a = w_o[0][:mla_out].astype(BF16)
    wo_b = w_o[0][mla_out:].astype(BF16)

    ct, st, cos, sin = _rope_tables(positions)
    a, k_mla, vt_mla = _mla_kv(x2, row(attn_norm[0]), w_ckv, row(kv_norm[0]), w_uk, w_uv, cos, sin)
    qt_mla, qt_mb, qt_mb_lo, k_mb, kmean, vt_mb = _in_proj(a, w_q, w_mb, ct, st, cos, sin)

    kmean = kmean.reshape(s // MOBA_BLOCK, N_HEADS * HEAD_DIM)
    bias_mb = _moba_gate(qt_mb, qt_mb_lo, kmean)
    onehot = jnp.asarray(np.arange(s)[:, None] // MOBA_BLOCK == np.arange(LANES)[None, :], dtype=BF16)

    out_mla = _flash(qt_mla, k_mla, vt_mla)
    out_mb = _flash(qt_mb, k_mb, vt_mb, bias=bias_mb, onehot=onehot)

    h1 = _out_proj(x2, out_mla, out_mb, wo_a, wo_b)
    h2 = _ffn(h1, row(ffn_norm[0]), w_gate[0].astype(BF16), w_up[0].astype(BF16), w_down[0].astype(BF16))
    out = _ple_final(h2, p2, row(ple_norm[0]), w_ple_gate[0].astype(BF16), w_ple_proj[0].astype(BF16),
                     row(final_norm))
    return out.reshape(b, s, d)
```

```python
import functools

import jax
import jax.numpy as jnp
from jax import lax
from jax.experimental import pallas as pl
from jax.experimental.pallas import tpu as pltpu

F32 = jnp.float32
BF16 = jnp.bfloat16

EPS = 1e-6
NEG = -1e30
M_INIT = -3.0e38
ROPE_THETA = 10000.0
LOG2E = 1.4426950408889634

LANES = 128
BF16_ROWS = 16
HEAD_DIM = 128
MLA_ROPE = 64
KV_RANK = 512
N_HEADS = 8
MOBA_BLOCK = 256
MOBA_BLOCK_LOG2 = MOBA_BLOCK.bit_length() - 1
MOBA_TOPK = 3
QK_WIDTH = 2 * LANES
ONES_ROWS = BF16_ROWS
VMEM_LIMIT = 56 * 1024 * 1024

_NT = (((1,), (1,)), ((), ()))


def _rms(x, g):
    ms = jnp.mean(x * x, axis=-1, keepdims=True)
    return x * lax.rsqrt(ms + EPS) * g


def _rope(slab, cos, sin):
    return slab * cos + pltpu.roll(slab, LANES // 2, axis=1) * sin


def _rope_t(x1, x2, c, s):
    return x1 * c - x2 * s, x2 * c + x1 * s


def _rope_table_kernel(pos_ref, inv_ref, ct_ref, st_ref, cos_ref, sin_ref):
    ang = inv_ref[...] * pos_ref[...].astype(F32)
    ct = jnp.cos(ang)
    st = jnp.sin(ang)
    ct_ref[...] = ct
    st_ref[...] = st
    c = ct.T
    s = st.T
    half, quart = LANES // 2, LANES // 4
    lane = lax.broadcasted_iota(jnp.int32, c.shape, 1)
    low = lane < half
    c_sw = pltpu.roll(c, half, axis=1)
    s_sw = pltpu.roll(s, half, axis=1)
    cos_ref[0] = jnp.where(low, c, c_sw)
    sin_ref[0] = jnp.where(low, -s, s_sw)
    c_q = pltpu.roll(c_sw, quart, axis=1)
    s_q = pltpu.roll(s_sw, quart, axis=1)
    first = lane < quart
    cos_ref[1] = jnp.where(first, c_sw, c_q)
    sin_ref[1] = jnp.where(first, -s_sw, s_q)


def _rope_tables(positions, bm=1024):
    s = positions.shape[-1]
    pos = positions.reshape(1, s)
    half_mb = HEAD_DIM // 2
    half_ml = MLA_ROPE // 2
    inv_mb = ROPE_THETA ** (-(jnp.arange(half_mb, dtype=F32) * 2.0 / HEAD_DIM))
    inv_ml = ROPE_THETA ** (-(jnp.arange(half_ml, dtype=F32) * 2.0 / MLA_ROPE))
    inv = jnp.concatenate([inv_mb, inv_ml, jnp.zeros((LANES - half_mb - half_ml,), F32)]).reshape(LANES, 1)
    t_out = jax.ShapeDtypeStruct((LANES, s), F32)
    r_out = jax.ShapeDtypeStruct((2, s, LANES), F32)
    return pl.pallas_call(
        _rope_table_kernel,
        grid=(s // bm,),
        in_specs=[pl.BlockSpec((1, bm), lambda i: (0, i)),
                  pl.BlockSpec((LANES, 1), lambda i: (0, 0))],
        out_specs=[pl.BlockSpec((LANES, bm), lambda i: (0, i))] * 2
                  + [pl.BlockSpec((2, bm, LANES), lambda i: (0, i, 0))] * 2,
        out_shape=[t_out, t_out, r_out, r_out],
        name="rope_tables",
    )(pos, inv)


def _proj_kernel(a_ref, wq_ref, wm_ref, ct_ref, st_ref, cos_ref, sin_ref,
                 qtl_ref, qtm_ref, qlo_ref, k_ref, km_ref, vt_ref, wt_scr, *, scale_mla, scale_mb):
    j = pl.program_id(0)
    i = pl.program_id(1)
    bm = a_ref.shape[0]
    half_mb, half_ml = HEAD_DIM // 2, MLA_ROPE // 2
    hq = HEAD_DIM + MLA_ROPE
    nq = wq_ref.shape[1]

    @pl.when((i == 0) & (j < 2))
    def _():
        wt_scr[0:nq, :] = wq_ref[...].T

    @pl.when((i == 0) & ((j == 2) | (j == 4)))
    def _():
        wt_scr[...] = wm_ref[...].T

    @pl.when(j < 2)
    def _():
        t = lax.dot_general(wt_scr[0:nq, :], a_ref[...], _NT, preferred_element_type=F32)
        c = ct_ref[half_mb:half_mb + half_ml, :]
        s = st_ref[half_mb:half_mb + half_ml, :]
        for h in range(nq // hq):
            r0, o0 = h * hq, h * QK_WIDTH
            qtl_ref[o0:o0 + HEAD_DIM, :] = (t[r0:r0 + HEAD_DIM, :] * scale_mla).astype(BF16)
            x1 = t[r0 + HEAD_DIM:r0 + HEAD_DIM + half_ml, :]
            x2 = t[r0 + HEAD_DIM + half_ml:r0 + hq, :]
            o1, o2 = _rope_t(x1, x2, c, s)
            qtl_ref[o0 + HEAD_DIM:o0 + HEAD_DIM + half_ml, :] = (o1 * scale_mla).astype(BF16)
            qtl_ref[o0 + HEAD_DIM + half_ml:o0 + hq, :] = (o2 * scale_mla).astype(BF16)
            qtl_ref[o0 + hq:o0 + QK_WIDTH, :] = jnp.zeros((QK_WIDTH - hq, bm), BF16)

    @pl.when(j == 2)
    def _():
        t = lax.dot_general(wt_scr[...], a_ref[...], _NT, preferred_element_type=F32)
        c = ct_ref[0:half_mb, :]
        s = st_ref[0:half_mb, :]
        for h in range(N_HEADS):
            r0 = h * HEAD_DIM
            o1, o2 = _rope_t(t[r0:r0 + half_mb, :], t[r0 + half_mb:r0 + HEAD_DIM, :], c, s)
            for off, o in ((r0, o1 * scale_mb), (r0 + half_mb, o2 * scale_mb)):
                hi = o.astype(BF16)
                qtm_ref[off:off + half_mb, :] = hi
                qlo_ref[off:off + half_mb, :] = (o - hi.astype(F32)).astype(BF16)

    @pl.when(j == 3)
    def _():
        acc = jnp.dot(a_ref[...], wm_ref[...], preferred_element_type=F32)
        nb = bm // MOBA_BLOCK
        for h in range(N_HEADS):
            r = _rope(acc[:, h * LANES:(h + 1) * LANES], cos_ref[0], sin_ref[0])
            k_ref[:, h * LANES:(h + 1) * LANES] = r.astype(BF16)
            km = r.reshape(nb, MOBA_BLOCK, LANES).sum(axis=1) * (1.0 / MOBA_BLOCK)
            km_ref[0, :, h * LANES:(h + 1) * LANES] = km

    @pl.when(j == 4)
    def _():
        vt = lax.dot_general(wt_scr[...], a_ref[...], _NT, preferred_element_type=F32)
        vt_ref[...] = vt.astype(BF16)


def _in_proj(a, wq, wm, ct, st, cos, sin, bm=1024):
    s, d = a.shape
    mw = N_HEADS * HEAD_DIM
    hq = HEAD_DIM + MLA_ROPE
    nq = wq.shape[1] // 2
    assert wq.shape[1] == N_HEADS * hq and wm.shape[1] == 3 * mw and nq <= mw
    nb = bm // MOBA_BLOCK
    ni = s // bm
    kern = functools.partial(_proj_kernel, scale_mla=float(hq ** -0.5 * LOG2E),
                             scale_mb=float(HEAD_DIM ** -0.5 * LOG2E))
    qw = N_HEADS * QK_WIDTH

    def rows(first, last):
        return lambda j, i: jnp.where(j < first, 0, jnp.where(j > last, ni - 1, i))

    q_i, m_i, k_i, v_i = rows(0, 1), rows(2, 2), rows(3, 3), rows(4, 4)
    return pl.pallas_call(
        kern,
        grid=(5, ni),
        in_specs=[pl.BlockSpec((bm, d), lambda j, i: (i, 0)),
                  pl.BlockSpec((d, nq), lambda j, i: (0, jnp.minimum(j, 1))),
                  pl.BlockSpec((d, mw), lambda j, i: (0, jnp.clip(j - 2, 0, 2))),
                  pl.BlockSpec((LANES, bm), lambda j, i: (0, i)),
                  pl.BlockSpec((LANES, bm), lambda j, i: (0, i)),
                  pl.BlockSpec((1, bm, LANES), lambda j, i: (0, i, 0)),
                  pl.BlockSpec((1, bm, LANES), lambda j, i: (0, i, 0))],
        out_specs=[pl.BlockSpec((qw // 2, bm), lambda j, i: (jnp.minimum(j, 1), q_i(j, i))),
                   pl.BlockSpec((mw, bm), lambda j, i: (0, m_i(j, i))),
                   pl.BlockSpec((mw, bm), lambda j, i: (0, m_i(j, i))),
                   pl.BlockSpec((bm, mw), lambda j, i: (k_i(j, i), 0)),
                   pl.BlockSpec((1, nb, mw), lambda j, i: (k_i(j, i), 0, 0)),
                   pl.BlockSpec((mw, bm), lambda j, i: (0, v_i(j, i)))],
        out_shape=[jax.ShapeDtypeStruct((qw, s), BF16),
                   jax.ShapeDtypeStruct((mw, s), BF16),
                   jax.ShapeDtypeStruct((mw, s), BF16),
                   jax.ShapeDtypeStruct((s, mw), BF16),
                   jax.ShapeDtypeStruct((ni, nb, mw), F32),
                   jax.ShapeDtypeStruct((mw, s), BF16)],
        scratch_shapes=[pltpu.VMEM((mw, d), BF16)],
        compiler_params=pltpu.CompilerParams(
            dimension_semantics=("arbitrary", "arbitrary"), vmem_limit_bytes=VMEM_LIMIT),
        name="in_proj",
    )(a, wq, wm, ct, st, cos, sin)


def _mla_kv_kernel(x_ref, g_ref, wc_ref, kvg_ref, wuk_ref, wuv_ref, cos_ref, sin_ref,
                   a_ref, k_ref, vt_ref, wuvt_scr):
    @pl.when(pl.program_id(0) == 0)
    def _():
        wuvt_scr[...] = wuv_ref[...].T

    a = _rms(x_ref[...], g_ref[...]).astype(BF16)
    a_ref[...] = a
    t = jnp.dot(a, wc_ref[...], preferred_element_type=F32)
    cn = _rms(t[:, :KV_RANK], kvg_ref[...]).astype(BF16)
    kn = jnp.dot(cn, wuk_ref[...], preferred_element_type=F32)
    vt = lax.dot_general(wuvt_scr[...], cn, _NT, preferred_element_type=F32)
    slab = t[:, KV_RANK:]
    quart = LANES // 4
    lane = lax.broadcasted_iota(jnp.int32, slab.shape, 1)
    partner = jnp.where(lane < quart, pltpu.roll(slab, LANES - quart, axis=1), pltpu.roll(slab, quart, axis=1))
    kpe = (slab * cos_ref[0] + partner * sin_ref[0]).astype(BF16)
    for h in range(N_HEADS):
        k_ref[:, (2 * h) * LANES:(2 * h + 1) * LANES] = kn[:, h * LANES:(h + 1) * LANES].astype(BF16)
        k_ref[:, (2 * h + 1) * LANES:(2 * h + 2) * LANES] = kpe
    vt_ref[...] = vt.astype(BF16)


def _mla_kv(x2, g, wc, kvg, wuk, wuv, cos, sin, bm=512):
    s, d = x2.shape
    full = lambda w: pl.BlockSpec(w.shape, lambda i: (0,) * w.ndim)
    return pl.pallas_call(
        _mla_kv_kernel,
        grid=(s // bm,),
        in_specs=[pl.BlockSpec((bm, d), lambda i: (i, 0)), full(g), full(wc), full(kvg), full(wuk), full(wuv),
                  pl.BlockSpec((1, bm, LANES), lambda i: (1, i, 0)),
                  pl.BlockSpec((1, bm, LANES), lambda i: (1, i, 0))],
        out_specs=[pl.BlockSpec((bm, d), lambda i: (i, 0)),
                   pl.BlockSpec((bm, N_HEADS * QK_WIDTH), lambda i: (i, 0)),
                   pl.BlockSpec((N_HEADS * HEAD_DIM, bm), lambda i: (0, i))],
        out_shape=[jax.ShapeDtypeStruct((s, d), BF16),
                   jax.ShapeDtypeStruct((s, N_HEADS * QK_WIDTH), BF16),
                   jax.ShapeDtypeStruct((N_HEADS * HEAD_DIM, s), BF16)],
        scratch_shapes=[pltpu.VMEM((wuv.shape[1], wuv.shape[0]), BF16)],
        compiler_params=pltpu.CompilerParams(
            dimension_semantics=("arbitrary",), vmem_limit_bytes=VMEM_LIMIT),
        name="mla_kv",
    )(x2, g, wc, kvg, wuk, wuv, cos, sin)


def _gate_kernel(qh_ref, ql_ref, km_ref, b_ref, *, nrow):
    i = pl.program_id(0)
    bm = qh_ref.shape[1]
    nblk = km_ref.shape[0]
    blk_row = lax.broadcasted_iota(jnp.int32, (nrow, bm), 0)
    qblk = lax.shift_right_logical(i * bm + lax.broadcasted_iota(jnp.int32, (nrow, bm), 1), MOBA_BLOCK_LOG2)
    rowf = blk_row.astype(F32)
    past = blk_row < qblk
    own = jnp.where(blk_row == qblk, 1.0, 0.0)
    zpad = jnp.zeros((nrow - nblk, LANES), F32)
    for h in range(N_HEADS):
        rows = slice(h * HEAD_DIM, (h + 1) * HEAD_DIM)
        km = km_ref[:, h * LANES:(h + 1) * LANES]
        if nrow > nblk:
            km = jnp.concatenate([km, zpad], axis=0)
        km_hi = km.astype(BF16)
        km_lo = (km - km_hi.astype(F32)).astype(BF16)
        qh = qh_ref[rows, :]
        gate = (jnp.dot(km_hi, qh, preferred_element_type=F32)
                + jnp.dot(km_hi, ql_ref[rows, :], preferred_element_type=F32)
                + jnp.dot(km_lo, qh, preferred_element_type=F32))
        g = jnp.where(past, gate, NEG)
        sel = own
        for t in range(MOBA_TOPK):
            m = jnp.max(g, axis=0, keepdims=True)
            idx = jnp.min(jnp.where(g == m, rowf, float(nrow)), axis=0, keepdims=True)
            pick = rowf == idx
            sel = jnp.where(pick, jnp.where(qblk > t, 1.0, sel), sel)
            g = jnp.where(pick, -jnp.inf, g)
        b_ref[h * nrow:(h + 1) * nrow, :] = jnp.where(sel > 0.5, 0.0, NEG).astype(BF16)


def _moba_gate(qt_hi, qt_lo, kmean, bm=512):
    mw, s = qt_hi.shape
    nblk = kmean.shape[0]
    nrow = -(-nblk // BF16_ROWS) * BF16_ROWS
    assert nrow <= LANES
    return pl.pallas_call(
        functools.partial(_gate_kernel, nrow=nrow),
        grid=(s // bm,),
        in_specs=[pl.BlockSpec((mw, bm), lambda i: (0, i)),
                  pl.BlockSpec((mw, bm), lambda i: (0, i)),
                  pl.BlockSpec((nblk, mw), lambda i: (0, 0))],
        out_specs=pl.BlockSpec((N_HEADS * nrow, bm), lambda i: (0, i)),
        out_shape=jax.ShapeDtypeStruct((N_HEADS * nrow, s), BF16),
        compiler_params=pltpu.CompilerParams(dimension_semantics=("arbitrary",)),
        name="moba_gate",
    )(qt_hi, qt_lo, kmean)


def _flash_kernel(*refs, ng, tg, tk, moba):
    if moba:
        q_ref, b_ref, k_ref, oh_ref, vt_ref, o_ref, qt_ref, s_scr, cm_scr, m_scr, l_scr, acc_scr = refs
        nrow = b_ref.shape[0]
        qt_ref[0:HEAD_DIM, :] = q_ref[...]
        qt_ref[HEAD_DIM:HEAD_DIM + nrow, :] = b_ref[...]
        if HEAD_DIM + nrow < QK_WIDTH:
            qt_ref[HEAD_DIM + nrow:QK_WIDTH, :] = jnp.zeros((QK_WIDTH - HEAD_DIM - nrow, qt_ref.shape[1]), BF16)
    else:
        qt_ref, k_ref, vt_ref, o_ref, s_scr, cm_scr, m_scr, l_scr, acc_scr = refs
    qi = pl.program_id(1)
    r = tk // tg
    m_scr[...] = jnp.full(m_scr.shape, M_INIT, F32)
    l_scr[...] = jnp.zeros(l_scr.shape, F32)
    acc_scr[...] = jnp.zeros(acc_scr.shape, F32)

    def start(buf, g, kj, need_cm=True):
        rows = pl.ds(pl.multiple_of(kj * tk, tk), tk)
        k = k_ref[rows, :]
        if moba:
            k = jnp.concatenate([k, oh_ref[rows, :]], axis=1)
        st = jnp.dot(k, qt_ref[:, g * tg:(g + 1) * tg], preferred_element_type=F32)
        s_scr[buf] = st
        if need_cm:
            cm_scr[buf] = jnp.max(st, axis=0, keepdims=True)

    def finish(buf, g, kj, diag_off):
        st = s_scr[buf]
        if diag_off is not None:
            key = lax.broadcasted_iota(jnp.int32, (tk, tg), 0)
            qry = lax.broadcasted_iota(jnp.int32, (tk, tg), 1) + diag_off
            st = jnp.where(key <= qry, st, NEG)
            cm = jnp.max(st, axis=0, keepdims=True)
        else:
            cm = cm_scr[buf]
        m_prev = m_scr[g]
        m_new = jnp.maximum(m_prev, cm)
        alpha = jnp.exp2(m_prev - m_new)
        p = jnp.exp2(st - m_new).astype(BF16)
        vt = vt_ref[:, pl.ds(pl.multiple_of(kj * tk, tk), tk)]
        vt1 = jnp.concatenate([vt, jnp.ones((ONES_ROWS, tk), BF16)], axis=0)
        pv = jnp.dot(vt1, p, preferred_element_type=F32)
        l_scr[g] = alpha * l_scr[g] + pv[HEAD_DIM:HEAD_DIM + 1, :]
        acc_scr[g] = alpha * acc_scr[g] + pv[:HEAD_DIM, :]
        m_scr[g] = m_new

    n_full = qi * (ng // r)
    tail = [(c, g, (g - c * r) * tg if g < (c + 1) * r else None)
            for c in range(ng // r) for g in range(c * r, ng)]

    start(0, 0, 0)

    def chunk(kj):
        for g in range(ng):
            if g + 1 < ng:
                start((g + 1) % 2, g + 1, kj)
            else:
                start(0, 0, kj + 1)
            finish(g % 2, g, kj, None)

    unroll = max(u for u in (1, 2, 4) if (ng // r) % u == 0)

    def body(t, c):
        for u in range(unroll):
            chunk(t * unroll + u)
        return c

    lax.fori_loop(0, n_full // unroll, body, 0)
    for i, (c, g, diag_off) in enumerate(tail):
        if i + 1 < len(tail):
            nc, ngp, noff = tail[i + 1]
            start((i + 1) % 2, ngp, n_full + nc, need_cm=noff is None)
        finish(i % 2, g, n_full + c, diag_off)
    for g in range(ng):
        o = acc_scr[g] / l_scr[g]
        o_ref[g * tg:(g + 1) * tg, :] = o.T.astype(o_ref.dtype)


def _flash(qt, k, vt, bias=None, onehot=None, ng=4, tg=512, tk=512):
    s = k.shape[0]
    tq = ng * tg
    moba = bias is not None
    assert ng % 2 == 0 and tk % tg == 0 and tq % tk == 0
    kern = functools.partial(_flash_kernel, ng=ng, tg=tg, tk=tk, moba=moba)
    scratch = [pltpu.VMEM((2, tk, tg), F32), pltpu.VMEM((2, 1, tg), F32),
               pltpu.VMEM((ng, 1, tg), F32), pltpu.VMEM((ng, 1, tg), F32),
               pltpu.VMEM((ng, HEAD_DIM, tg), F32)]
    vt_spec = pl.BlockSpec((HEAD_DIM, s), lambda h, i: (h, 0))
    if moba:
        nrow = bias.shape[0] // N_HEADS
        in_specs = [pl.BlockSpec((HEAD_DIM, tq), lambda h, i: (h, i)),
                    pl.BlockSpec((nrow, tq), lambda h, i: (h, i)),
                    pl.BlockSpec((s, HEAD_DIM), lambda h, i: (0, h)),
                    pl.BlockSpec((s, LANES), lambda h, i: (0, 0)),
                    vt_spec]
        operands = (qt, bias, k, onehot, vt)
        scratch = [pltpu.VMEM((QK_WIDTH, tq), BF16)] + scratch
    else:
        in_specs = [pl.BlockSpec((QK_WIDTH, tq), lambda h, i: (h, i)),
                    pl.BlockSpec((s, QK_WIDTH), lambda h, i: (0, h)),
                    vt_spec]
        operands = (qt, k, vt)
    return pl.pallas_call(
        kern,
        grid=(N_HEADS, s // tq),
        in_specs=in_specs,
        out_specs=pl.BlockSpec((tq, HEAD_DIM), lambda h, i: (i, h)),
        out_shape=jax.ShapeDtypeStruct((s, N_HEADS * HEAD_DIM), BF16),
        scratch_shapes=scratch,
        compiler_params=pltpu.CompilerParams(
            dimension_semantics=("arbitrary", "arbitrary"), vmem_limit_bytes=VMEM_LIMIT),
        name="flash_attn",
    )(*operands)


def _wo_kernel(x_ref, a_ref, b_ref, wa_ref, wb_ref, o_ref):
    o_ref[...] = (x_ref[...]
                  + jnp.dot(a_ref[...], wa_ref[...], preferred_element_type=F32)
                  + jnp.dot(b_ref[...], wb_ref[...], preferred_element_type=F32))


def _out_proj(x2, oa, ob, wa, wb, bm=512):
    s, d = x2.shape
    ka, kb = oa.shape[1], ob.shape[1]
    return pl.pallas_call(
        _wo_kernel,
        grid=(s // bm,),
        in_specs=[pl.BlockSpec((bm, d), lambda i: (i, 0)),
                  pl.BlockSpec((bm, ka), lambda i: (i, 0)),
                  pl.BlockSpec((bm, kb), lambda i: (i, 0)),
                  pl.BlockSpec((ka, d), lambda i: (0, 0)),
                  pl.BlockSpec((kb, d), lambda i: (0, 0))],
        out_specs=pl.BlockSpec((bm, d), lambda i: (i, 0)),
        out_shape=jax.ShapeDtypeStruct((s, d), F32),
        compiler_params=pltpu.CompilerParams(
            dimension_semantics=("arbitrary",), vmem_limit_bytes=VMEM_LIMIT),
        name="out_proj",
    )(x2, oa, ob, wa, wb)


def _ffn_kernel(h_ref, g_ref, wg_ref, wu_ref, wd_ref, o_ref, f_scr):
    k = pl.program_id(1)

    @pl.when(k == 0)
    def _():
        h = h_ref[...]
        f_scr[...] = _rms(h, g_ref[...]).astype(BF16)
        o_ref[...] = h

    f = f_scr[...]
    gt = jnp.dot(f, wg_ref[...], preferred_element_type=F32)
    up = jnp.dot(f, wu_ref[...], preferred_element_type=F32)
    act = (gt * jax.nn.sigmoid(gt) * up).astype(BF16)
    nchunk = wd_ref.shape[0]
    for n0 in range(0, wd_ref.shape[1], nchunk):
        o_ref[:, n0:n0 + nchunk] += jnp.dot(act, wd_ref[:, n0:n0 + nchunk], preferred_element_type=F32)


def _ffn(h, g, wg, wu, wd, bm=1024, bf=512):
    s, d = h.shape
    f = wg.shape[1]
    return pl.pallas_call(
        _ffn_kernel,
        grid=(s // bm, f // bf),
        in_specs=[pl.BlockSpec((bm, d), lambda i, k: (i, 0)),
                  pl.BlockSpec((1, d), lambda i, k: (0, 0)),
                  pl.BlockSpec((d, bf), lambda i, k: (0, k)),
                  pl.BlockSpec((d, bf), lambda i, k: (0, k)),
                  pl.BlockSpec((bf, d), lambda i, k: (k, 0))],
        out_specs=pl.BlockSpec((bm, d), lambda i, k: (i, 0)),
        out_shape=jax.ShapeDtypeStruct((s, d), F32),
        scratch_shapes=[pltpu.VMEM((bm, d), BF16)],
        compiler_params=pltpu.CompilerParams(
            dimension_semantics=("arbitrary", "arbitrary"), vmem_limit_bytes=VMEM_LIMIT),
        name="swiglu",
    )(h, g, wg, wu, wd)


def _ple_kernel(h_ref, p_ref, gp_ref, wpg_ref, wpe_ref, gf_ref, o_ref):
    h = h_ref[...]
    r = _rms(h, gp_ref[...]).astype(BF16)
    gate = jax.nn.sigmoid(jnp.dot(r, wpg_ref[...], preferred_element_type=F32))
    pe = jnp.dot(p_ref[...].astype(BF16), wpe_ref[...], preferred_element_type=F32)
    o_ref[...] = _rms(h + gate * pe, gf_ref[...])


def _ple_final(h, p2, gp, wpg, wpe, gf, bm=512):
    s, d = h.shape
    pd = p2.shape[1]
    full = lambda a: pl.BlockSpec(a.shape, lambda i: (0,) * a.ndim)
    return pl.pallas_call(
        _ple_kernel,
        grid=(s // bm,),
        in_specs=[pl.BlockSpec((bm, d), lambda i: (i, 0)),
                  pl.BlockSpec((bm, pd), lambda i: (i, 0)),
                  full(gp), full(wpg), full(wpe), full(gf)],
        out_specs=pl.BlockSpec((bm, d), lambda i: (i, 0)),
        out_shape=jax.ShapeDtypeStruct((s, d), F32),
        compiler_params=pltpu.CompilerParams(
            dimension_semantics=("arbitrary",), vmem_limit_bytes=VMEM_LIMIT),
        name="ple_final",
    )(h, p2, gp, wpg, wpe, gf)


def kernel(x, p, positions, attn_norm, w_in, kv_norm, w_ukv, w_o, ffn_norm, w_gate, w_up, w_down,
           ple_norm, w_ple_gate, w_ple_proj, final_norm):
    b, s, d = x.shape
    assert b == 1 and p.shape[0] == 1 and s % 2048 == 0
    x2 = x.reshape(s, d)
    p2 = p.reshape(s, p.shape[-1])
    row = lambda v: v.reshape(1, -1).astype(F32)

    q_cols = N_HEADS * (HEAD_DIM + MLA_ROPE)
    c1 = q_cols + KV_RANK
    c2 = c1 + MLA_ROPE
    w_in0 = w_in[0].astype(BF16)
    w_q = w_in0[:, :q_cols]
    w_ckv = jnp.pad(w_in0[:, q_cols:c2], ((0, 0), (0, LANES - MLA_ROPE)))
    w_mb = w_in0[:, c2:]
    wukv = w_ukv[0].reshape(KV_RANK, N_HEADS, 2 * HEAD_DIM)
    w_uk = wukv[:, :, :HEAD_DIM].reshape(KV_RANK, -1).astype(BF16)
    w_uv = wukv[:, :, HEAD_DIM:].reshape(KV_RANK, -1).astype(BF16)
    mla_out = N_HEADS * HEAD_DIM
    wo_a = w_o[0][:mla_out].astype(BF16)
    wo_b = w_o[0][mla_out:].astype(BF16)

    ct, st, cos, sin = _rope_tables(positions)
    a, k_mla, vt_mla = _mla_kv(x2, row(attn_norm[0]), w_ckv, row(kv_norm[0]), w_uk, w_uv, cos, sin)
    qt_mla, qt_mb, qt_mb_lo, k_mb, kmean, vt_mb = _in_proj(a, w_q, w_mb, ct, st, cos, sin)

    kmean = kmean.reshape(s // MOBA_BLOCK, N_HEADS * HEAD_DIM)
    bias_mb = _moba_gate(qt_mb, qt_mb_lo, kmean)
    onehot = (jnp.arange(s, dtype=jnp.int32)[:, None] // MOBA_BLOCK
              == jnp.arange(LANES, dtype=jnp.int32)[None, :]).astype(BF16)

    out_mla = _flash(qt_mla, k_mla, vt_mla)
    out_mb = _flash(qt_mb, k_mb, vt_mb, bias=bias_mb, onehot=onehot)

    h1 = _out_proj(x2, out_mla, out_mb, wo_a, wo_b)
    h2 = _ffn(h1, row(ffn_norm[0]), w_gate[0].astype(BF16), w_up[0].astype(BF16), w_down[0].astype(BF16))
    out = _ple_final(h2, p2, row(ple_norm[0]), w_ple_gate[0].astype(BF16), w_ple_proj[0].astype(BF16),
                     row(final_norm))
    return out.reshape(b, s, d)
```
